```python
import math, functools
import jax, jax.numpy as jnp
from jax import lax
import numpy as np

D_MODEL = 1024
BATCH = 16
SEQ = 2048
DEPTH = 1
DEC_BATCH = 128
DEC_SEQ = 8
PAST_LEN = 8192
PAGE_SIZE = 128

SB_HEADS = 8
SB_HEAD_DIM = 64
SB_WIDTH = SB_HEADS * SB_HEAD_DIM
DF_HEADS = 4
DF_HEAD_DIM = 64
DF_WIDTH = DF_HEADS * 2 * DF_HEAD_DIM
IN_COLS = 3 * SB_WIDTH + 3 * DF_WIDTH + 2 * D_MODEL
N_GROUPS = 4
EXPERTS_PER_GROUP = 8
N_EXPERTS = N_GROUPS * EXPERTS_PER_GROUP
D_EXPERT = 256
TOP_K_IN_GROUP = 2
PLE_DIM = 256
ROPE_THETA = 10000.0
Q_BLOCK = 128
RMS_EPS = 1e-6

kernel_name = "hybrid_stickbreak_diffattn_hmoe_step"


def rmsnorm(x, g):
    xf = x.astype(jnp.float32)
    y = xf * lax.rsqrt(jnp.mean(xf * xf, axis=-1, keepdims=True) + RMS_EPS)
    return (y * g.astype(jnp.float32)).astype(x.dtype)


def rope(x, pos):
    hd = x.shape[-1]
    half = hd // 2
    inv = 1.0 / (ROPE_THETA ** (jnp.arange(half, dtype=jnp.float32) * 2.0 / hd))
    ang = pos.astype(jnp.float32)[:, None] * inv[None, :]
    cos = jnp.cos(ang)[:, None, :].astype(x.dtype)
    sin = jnp.sin(ang)[:, None, :].astype(x.dtype)
    x1, x2 = x[..., :half], x[..., half:]
    return jnp.concatenate([x1 * cos - x2 * sin, x2 * cos + x1 * sin], axis=-1)


def project(h, w_in, pos):
    B, T, _ = h.shape
    z = h @ w_in
    sizes = [SB_WIDTH] * 3 + [DF_WIDTH] * 3 + [D_MODEL]
    splits = [int(s) for s in np.cumsum(sizes)]
    q_sb, k_sb, v_sb, q_df, k_df, v_df, ga, gb = jnp.split(z, splits, axis=-1)
    q_sb = q_sb.reshape(B, T, SB_HEADS, SB_HEAD_DIM)
    k_sb = k_sb.reshape(B, T, SB_HEADS, SB_HEAD_DIM)
    v_sb = v_sb.reshape(B, T, SB_HEADS, SB_HEAD_DIM)

    def rot(t):
        t = rope(t.reshape(B, T, 2 * DF_HEADS, DF_HEAD_DIM), pos)
        return t.reshape(B, T, DF_HEADS, 2 * DF_HEAD_DIM)

    q_df = rot(q_df)
    k_df = rot(k_df)
    v_df = v_df.reshape(B, T, DF_HEADS, 2 * DF_HEAD_DIM)
    return q_sb, k_sb, v_sb, q_df, k_df, v_df, ga, gb


def stick_breaking(q, k, v, q_pos, k_pos):
    z = jnp.einsum('bqhd,bkhd->bhqk', q, k).astype(jnp.float32) * (SB_HEAD_DIM ** -0.5)
    mask = k_pos[None, :] < q_pos[:, None]
    log_beta = jax.nn.log_sigmoid(z)
    log_1mb = jnp.where(mask, jax.nn.log_sigmoid(-z), 0.0)
    log_rem = lax.cumsum(log_1mb, axis=3, reverse=True) - log_1mb
    w = jnp.where(mask, jnp.exp(log_beta + log_rem), 0.0)
    return jnp.einsum('bhqk,bkhd->bqhd', w.astype(v.dtype), v)


def diff_attention(q, k, v, q_pos, k_pos, lam):
    q1, q2 = jnp.split(q, 2, axis=-1)
    k1, k2 = jnp.split(k, 2, axis=-1)
    mask = k_pos[None, :] <= q_pos[:, None]

    def probs(qa, ka):
        s = jnp.einsum('bqhd,bkhd->bhqk', qa, ka).astype(jnp.float32) * (DF_HEAD_DIM ** -0.5)
        return jax.nn.softmax(jnp.where(mask, s, -jnp.inf), axis=-1)

    a = probs(q1, k1) - lam * probs(q2, k2)
    return jnp.einsum('bhqk,bkhd->bqhd', a.astype(v.dtype), v)


def prompt_blocks(attn, q, k, v, pos):
    outs = []
    for blk in range(q.shape[1] // Q_BLOCK):
        s, e = blk * Q_BLOCK, (blk + 1) * Q_BLOCK
        outs.append(attn(q[:, s:e], k[:, :e], v[:, :e], pos[s:e], pos[:e]))
    return jnp.concatenate(outs, axis=1)


def hier_moe(h, w_grp, w_rtr, w_gate, w_up, w_down):
    shp = h.shape
    t = h.reshape(-1, shp[-1])
    glog = (t @ w_grp).astype(jnp.float32)
    g_idx = jnp.argmax(glog, axis=-1)
    p_g = jnp.take_along_axis(jax.nn.softmax(glog, axis=-1), g_idx[:, None], axis=-1)
    elog = jnp.einsum('td,gde->tge', t, w_rtr).astype(jnp.float32)
    elog = jnp.take_along_axis(elog, g_idx[:, None, None], axis=1)[:, 0]
    top_v, top_i = lax.top_k(elog, TOP_K_IN_GROUP)
    w = jax.nn.softmax(top_v, axis=-1) * p_g
    flat = g_idx[:, None] * EXPERTS_PER_GROUP + top_i
    comb = jnp.sum(jax.nn.one_hot(flat, N_EXPERTS, dtype=jnp.float32) * w[..., None], axis=1).astype(t.dtype)
    out = jnp.zeros_like(t)
    for e in range(N_EXPERTS):
        hid = jax.nn.silu(t @ w_gate[e]) * (t @ w_up[e])
        out = out + comb[:, e:e + 1] * (hid @ w_down[e])
    return out.reshape(shp)


def finish_layer(x, o_sb, o_df, ga, gb, p, lam_init, g_subln, w_sb, w_df, w_o, g_ffn,
                 w_grp, w_rtr, w_gate, w_up, w_down, g_ple, w_ple_gate, w_ple_proj):
    B, T, _ = x.shape
    o_df = rmsnorm(o_df, g_subln) * (1.0 - lam_init)
    merged = (jax.nn.sigmoid(ga) * (o_sb.reshape(B, T, SB_WIDTH) @ w_sb)
              + jax.nn.sigmoid(gb) * (o_df.reshape(B, T, DF_WIDTH) @ w_df))
    x = x + merged @ w_o
    x = x + hier_moe(rmsnorm(x, g_ffn), w_grp, w_rtr, w_gate, w_up, w_down)
    x = x + jax.nn.sigmoid(rmsnorm(x, g_ple) @ w_ple_gate) * (p @ w_ple_proj)
    return x


def setup_inputs(seed: int = 0) -> dict:
    key = jax.random.key(seed)
    k = jax.random.split(key, 32)

    def nrm(i, shape, scale):
        return jax.random.normal(k[i], shape, jnp.float32) * scale

    def gain(i, shape):
        return 1.0 + nrm(i, shape, 0.02)

    n_pages = PAST_LEN // PAGE_SIZE
    n_used = DEC_BATCH * n_pages
    n_pool = n_used + n_used // 4
    page_table = jax.random.permutation(k[6], n_pool)[:n_used].reshape(DEC_BATCH, n_pages).astype(jnp.int32)
    return {
        "x_prompt": nrm(0, (BATCH, SEQ, D_MODEL), 1.0),
        "x_sample": nrm(1, (DEC_BATCH, DEC_SEQ, D_MODEL), 1.0),
        "p_prompt": nrm(2, (DEPTH, BATCH, SEQ, PLE_DIM), 1.0),
        "p_sample": nrm(3, (DEPTH, DEC_BATCH, DEC_SEQ, PLE_DIM), 1.0),
        "cache_sb": nrm(4, (n_pool, DEPTH, PAGE_SIZE, 2, SB_HEADS, SB_HEAD_DIM), 1.0),
        "cache_df": nrm(5, (n_pool, DEPTH, PAGE_SIZE, 2, DF_HEADS, 2 * DF_HEAD_DIM), 1.0),
        "page_table": page_table,
        "g_mix": gain(7, (DEPTH, D_MODEL)),
        "w_in": nrm(8, (DEPTH, D_MODEL, IN_COLS), D_MODEL ** -0.5),
        "lam_q1": nrm(9, (DEPTH, DF_HEAD_DIM), 0.1),
        "lam_k1": nrm(10, (DEPTH, DF_HEAD_DIM), 0.1),
        "lam_q2": nrm(11, (DEPTH, DF_HEAD_DIM), 0.1),
        "lam_k2": nrm(12, (DEPTH, DF_HEAD_DIM), 0.1),
        "g_subln": gain(13, (DEPTH, 2 * DF_HEAD_DIM)),
        "w_sb": nrm(14, (DEPTH, SB_WIDTH, D_MODEL), SB_WIDTH ** -0.5),
        "w_df": nrm(15, (DEPTH, DF_WIDTH, D_MODEL), DF_WIDTH ** -0.5),
        "w_o": nrm(16, (DEPTH, D_MODEL, D_MODEL), D_MODEL ** -0.5),
        "g_ffn": gain(17, (DEPTH, D_MODEL)),
        "w_grp": nrm(18, (DEPTH, D_MODEL, N_GROUPS), D_MODEL ** -0.5),
        "w_rtr": nrm(19, (DEPTH, N_GROUPS, D_MODEL, EXPERTS_PER_GROUP), D_MODEL ** -0.5),
        "w_gate": nrm(20, (DEPTH, N_EXPERTS, D_MODEL, D_EXPERT), D_MODEL ** -0.5),
        "w_up": nrm(21, (DEPTH, N_EXPERTS, D_MODEL, D_EXPERT), D_MODEL ** -0.5),
        "w_down": nrm(22, (DEPTH, N_EXPERTS, D_EXPERT, D_MODEL), D_EXPERT ** -0.5),
        "g_ple": gain(23, (DEPTH, D_MODEL)),
        "w_ple_gate": nrm(24, (DEPTH, D_MODEL, D_MODEL), D_MODEL ** -0.5),
        "w_ple_proj": nrm(25, (DEPTH, PLE_DIM, D_MODEL), PLE_DIM ** -0.5),
        "g_final": gain(26, (D_MODEL,)),
    }


def reference(x_prompt, x_sample, p_prompt, p_sample, cache_sb, cache_df, page_table,
              g_mix, w_in, lam_q1, lam_k1, lam_q2, lam_k2, g_subln, w_sb, w_df, w_o,
              g_ffn, w_grp, w_rtr, w_gate, w_up, w_down, g_ple, w_ple_gate, w_ple_proj, g_final):
    pos_p = jnp.arange(x_prompt.shape[1], dtype=jnp.int32)
    pos_s = PAST_LEN + jnp.arange(x_sample.shape[1], dtype=jnp.int32)
    pos_cat = jnp.concatenate([jnp.arange(PAST_LEN, dtype=jnp.int32), pos_s])
    xp, xs = x_prompt, x_sample
    sb_p, df_p, sb_s, df_s = [], [], [], []
    for i in range(DEPTH):
        lam_init = 0.8 - 0.6 * math.exp(-0.3 * i)
        lam = (jnp.exp(jnp.sum(lam_q1[i].astype(jnp.float32) * lam_k1[i].astype(jnp.float32)))
               - jnp.exp(jnp.sum(lam_q2[i].astype(jnp.float32) * lam_k2[i].astype(jnp.float32)))
               + lam_init)
        diff_i = functools.partial(diff_attention, lam=lam)
        rest = (lam_init, g_subln[i], w_sb[i], w_df[i], w_o[i], g_ffn[i], w_grp[i], w_rtr[i],
                w_gate[i], w_up[i], w_down[i], g_ple[i], w_ple_gate[i], w_ple_proj[i])

        h = rmsnorm(xp, g_mix[i])
        q_sb, k_sb, v_sb, q_df, k_df, v_df, ga, gb = project(h, w_in[i], pos_p)
        o_sb = prompt_blocks(stick_breaking, q_sb, k_sb, v_sb, pos_p)
        o_df = prompt_blocks(diff_i, q_df, k_df, v_df, pos_p)
        xp = finish_layer(xp, o_sb, o_df, ga, gb, p_prompt[i], *rest)
        sb_p.append(jnp.stack([k_sb, v_sb], axis=2))
        df_p.append(jnp.stack([k_df, v_df], axis=2))

        h = rmsnorm(xs, g_mix[i])
        q_sb, k_sb, v_sb, q_df, k_df, v_df, ga, gb = project(h, w_in[i], pos_s)

        def one_seq(args, i=i, diff_i=diff_i):
            qs, ks_, vs, qd, kd, vd, pt = args
            past_sb = cache_sb[pt, i].reshape(-1, 2, SB_HEADS, SB_HEAD_DIM)
            past_df = cache_df[pt, i].reshape(-1, 2, DF_HEADS, 2 * DF_HEAD_DIM)
            k1 = jnp.concatenate([past_sb[:, 0], ks_], axis=0)[None]
            v1 = jnp.concatenate([past_sb[:, 1], vs], axis=0)[None]
            k2 = jnp.concatenate([past_df[:, 0], kd], axis=0)[None]
            v2 = jnp.concatenate([past_df[:, 1], vd], axis=0)[None]
            o1 = stick_breaking(qs[None], k1, v1, pos_s, pos_cat)[0]
            o2 = diff_i(qd[None], k2, v2, pos_s, pos_cat)[0]
            return o1, o2

        o_sb, o_df = lax.map(one_seq, (q_sb, k_sb, v_sb, q_df, k_df, v_df, page_table))
        xs = finish_layer(xs, o_sb, o_df, ga, gb, p_sample[i], *rest)
        sb_s.append(jnp.stack([k_sb, v_sb], axis=2))
        df_s.append(jnp.stack([k_df, v_df], axis=2))

    y_prompt = rmsnorm(xp, g_final)
    y_sample = rmsnorm(xs, g_final)
    new_sb_prompt = jnp.stack(sb_p, axis=1)
    new_df_prompt = jnp.stack(df_p, axis=1)
    new_sb_sample = jnp.stack(sb_s, axis=1)
    new_df_sample = jnp.stack(df_s, axis=1)
    return (y_prompt, y_sample, new_sb_prompt, new_df_prompt, new_sb_sample, new_df_sample)
```

```python
import functools
import math

import jax
import jax.numpy as jnp
from jax import lax
from jax.experimental import pallas as pl
from jax.experimental.pallas import tpu as pltpu

F32 = jnp.float32
BF16 = jnp.bfloat16

RMS_EPS = 1e-6
ROPE_THETA = 10000.0
D_MODEL = 1024
SB_HEADS = 8
DF_HEADS = 4
HEAD_DIM = 64
SB_WIDTH = SB_HEADS * HEAD_DIM
DF_WIDTH = DF_HEADS * 2 * HEAD_DIM
N_GROUPS = 4
EXPERTS_PER_GROUP = 8
N_EXPERTS = N_GROUPS * EXPERTS_PER_GROUP
D_EXPERT = 256
LANES = 128
NEG_BIG = -1e30
VMEM_LIMIT = 56 * 1024 * 1024


def _cparams(sem):
    return pltpu.CompilerParams(dimension_semantics=sem, vmem_limit_bytes=VMEM_LIMIT)


def _rms(x, g):
    ms = jnp.mean(x * x, axis=-1, keepdims=True)
    return x * lax.rsqrt(ms + RMS_EPS) * g


def _dot(a, b):
    return jnp.dot(a, b, preferred_element_type=F32)


def _dot_nt(a, b):
    return lax.dot_general(a, b, (((1,), (1,)), ((), ())), preferred_element_type=F32)


def _proj_kernel(x_ref, g_ref, w_ref, cos_ref, sa_ref, sb_ref,
                 qsb_ref, qdf_ref, nsb_ref, ndf_ref, sb16_ref, df16_ref, gate_ref):
    h = _rms(x_ref[...], g_ref[...]).astype(BF16)
    cos, sa, sb = cos_ref[...], sa_ref[...], sb_ref[...]

    def mm(c):
        return _dot(h, w_ref[:, c * 1024:(c + 1) * 1024])

    def rope(t):
        parts = []
        for c in range(t.shape[1] // LANES):
            u = t[:, c * LANES:(c + 1) * LANES]
            parts.append(u * cos + pltpu.roll(u, 96, 1) * sa + pltpu.roll(u, 32, 1) * sb)
        return jnp.concatenate(parts, axis=1)

    z = mm(0)
    qsb_ref[...] = (z[:, :512] * 0.125).astype(qsb_ref.dtype)
    nsb_ref[:, :512] = z[:, 512:]
    sb16_ref[:, :512] = z[:, 512:].astype(BF16)
    z = mm(1)
    nsb_ref[:, 512:] = z[:, :512]
    sb16_ref[:, 512:] = z[:, :512].astype(BF16)
    qdf_ref[...] = (rope(z[:, 512:]) * 0.125).astype(qdf_ref.dtype)
    z = mm(2)
    kd = rope(z[:, :512])
    ndf_ref[:, :512] = kd
    df16_ref[:, :512] = kd.astype(BF16)
    ndf_ref[:, 512:] = z[:, 512:]
    df16_ref[:, 512:] = z[:, 512:].astype(BF16)
    gate_ref[:, :1024] = jax.nn.sigmoid(mm(3))
    gate_ref[:, 1024:] = jax.nn.sigmoid(mm(4))


def _project(x, g, w_in16, tables, period_blocks, q_dtype, tm):
    n = x.shape[0]
    cos, sa, sb = tables
    row = lambda i: (i, 0)
    const = lambda i: (0, 0)
    tab = lambda i: (i % period_blocks, 0)
    out_shape = (
        jax.ShapeDtypeStruct((n, 512), q_dtype), jax.ShapeDtypeStruct((n, 512), q_dtype),
        jax.ShapeDtypeStruct((n, 1024), F32), jax.ShapeDtypeStruct((n, 1024), F32),
        jax.ShapeDtypeStruct((n, 1024), BF16), jax.ShapeDtypeStruct((n, 1024), BF16),
        jax.ShapeDtypeStruct((n, 2048), F32))
    return pl.pallas_call(
        _proj_kernel,
        grid=(n // tm,),
        in_specs=[pl.BlockSpec((tm, D_MODEL), row), pl.BlockSpec((1, D_MODEL), const),
                  pl.BlockSpec(w_in16.shape, const, pipeline_mode=pl.Buffered(1)),
                  pl.BlockSpec((tm, LANES), tab), pl.BlockSpec((tm, LANES), tab),
                  pl.BlockSpec((tm, LANES), tab)],
        out_specs=(pl.BlockSpec((tm, 512), row), pl.BlockSpec((tm, 512), row),
                   pl.BlockSpec((tm, 1024), row), pl.BlockSpec((tm, 1024), row),
                   pl.BlockSpec((tm, 1024), row), pl.BlockSpec((tm, 1024), row),
                   pl.BlockSpec((tm, 2048), row)),
        out_shape=out_shape,
        compiler_params=_cparams(("parallel",)),
        name="proj",
    )(x, g, w_in16, cos, sa, sb)


def _rope_tables(pos):
    half = HEAD_DIM // 2
    inv = 1.0 / (ROPE_THETA ** (jnp.arange(half, dtype=F32) * 2.0 / HEAD_DIM))
    ang = pos.astype(F32)[:, None] * inv[None, :]
    cos, sin = jnp.cos(ang), jnp.sin(ang)
    zero = jnp.zeros_like(sin)
    cos64 = jnp.concatenate([cos, cos], axis=1)
    sa64 = jnp.concatenate([-sin, zero], axis=1)
    sb64 = jnp.concatenate([zero, sin], axis=1)
    rep = lambda t: jnp.concatenate([t, t], axis=1)
    return rep(cos64), rep(sa64), rep(sb64)


def _sb_prompt_kernel(q_ref, k_ref, v_ref, o_ref, *, tq):
    i = pl.program_id(2)
    lane = lax.broadcasted_iota(jnp.int32, (tq, LANES), 1)
    row = lax.broadcasted_iota(jnp.int32, (tq, tq), 0)
    col = lax.broadcasted_iota(jnp.int32, (tq, tq), 1)
    later = (row > col).astype(BF16)
    causal = col < row
    q = q_ref[...].astype(F32)
    accs = []
    for hh in range(2):
        qh = jnp.where((lane >= HEAD_DIM * hh) & (lane < HEAD_DIM * (hh + 1)), q, 0.0).astype(BF16)

        def tile(kb, c, acc, masked, qh=qh):
            start = pl.multiple_of(kb * tq, tq)
            k = k_ref[pl.ds(start, tq), :]
            v = v_ref[pl.ds(start, tq), :]
            z = _dot_nt(qh, k)
            soft = jnp.log(1.0 + jnp.exp(-jnp.abs(z)))
            log_beta = jnp.minimum(z, 0.0) - soft
            log_1mb = log_beta - z
            if masked:
                log_1mb = jnp.where(causal, log_1mb, 0.0)
            rem = _dot(log_1mb.astype(BF16), later)
            w = jnp.exp(log_beta + rem + c)
            if masked:
                w = jnp.where(causal, w, 0.0)
            acc = acc + _dot(w.astype(BF16), v)
            c = c + jnp.sum(log_1mb, axis=1, keepdims=True)
            return c, acc

        c, acc = tile(i, jnp.zeros((tq, 1), F32), jnp.zeros((tq, LANES), F32), True)
        c, acc = lax.fori_loop(0, i, lambda s, st: tile(i - 1 - s, st[0], st[1], False), (c, acc))
        accs.append(acc)
    o_ref[...] = jnp.where(lane < HEAD_DIM, accs[0], accs[1]).astype(o_ref.dtype)


def _sb_prompt(q, kv16, batch, seq, tq):
    nq = seq // tq
    return pl.pallas_call(
        functools.partial(_sb_prompt_kernel, tq=tq),
        grid=(batch, SB_WIDTH // LANES, nq),
        in_specs=[pl.BlockSpec((tq, LANES), lambda b, p, i: (b * nq + i, p)),
                  pl.BlockSpec((seq, LANES), lambda b, p, i: (b, p)),
                  pl.BlockSpec((seq, LANES), lambda b, p, i: (b, SB_WIDTH // LANES + p))],
        out_specs=pl.BlockSpec((tq, LANES), lambda b, p, i: (b * nq + i, p)),
        out_shape=jax.ShapeDtypeStruct((batch * seq, SB_WIDTH), BF16),
        compiler_params=_cparams(("parallel", "parallel", "arbitrary")),
        name="sb_prompt",
    )(q, kv16, kv16)


def _lambda(lam_ref, lam_init):
    lp = lam_ref[...]
    a = jnp.sum(lp[0:1] * lp[1:2], axis=1, keepdims=True)
    b = jnp.sum(lp[2:3] * lp[3:4], axis=1, keepdims=True)
    return jnp.exp(a) - jnp.exp(b) + lam_init


def _df_prompt_kernel(lam_ref, g_ref, q_ref, k_ref, v_ref, o_ref, *, tq, lam_init):
    i = pl.program_id(2)
    lane = lax.broadcasted_iota(jnp.int32, (tq, LANES), 1)
    row = lax.broadcasted_iota(jnp.int32, (tq, tq), 0)
    col = lax.broadcasted_iota(jnp.int32, (tq, tq), 1)
    causal = col <= row
    q = q_ref[...].astype(F32)
    halves = (jnp.where(lane < HEAD_DIM, q, 0.0).astype(BF16), jnp.where(lane >= HEAD_DIM, q, 0.0).astype(BF16))

    def tile(kb, st, masked):
        start = pl.multiple_of(kb * tq, tq)
        k = k_ref[pl.ds(start, tq), :]
        v = v_ref[pl.ds(start, tq), :]
        new = []
        for qx, (m, l, acc) in zip(halves, st):
            s = _dot_nt(qx, k)
            if masked:
                s = jnp.where(causal, s, NEG_BIG)
            m_new = jnp.maximum(m, jnp.max(s, axis=1, keepdims=True))
            alpha = jnp.exp(m - m_new)
            p = jnp.exp(s - m_new)
            l = l * alpha + jnp.sum(p, axis=1, keepdims=True)
            acc = acc * alpha + _dot(p.astype(BF16), v)
            new.append((m_new, l, acc))
        return tuple(new)

    init = (jnp.full((tq, 1), NEG_BIG, F32), jnp.zeros((tq, 1), F32), jnp.zeros((tq, LANES), F32))
    st = tile(i, (init, init), True)
    st = lax.fori_loop(0, i, lambda s, c: tile(i - 1 - s, c, False), st)
    (_, l1, a1), (_, l2, a2) = st
    o = a1 / l1 - _lambda(lam_ref, lam_init) * (a2 / l2)
    o_ref[...] = (_rms(o, g_ref[...]) * (1.0 - lam_init)).astype(o_ref.dtype)


def _df_prompt(lam_p, g_subln, q, kv16, batch, seq, tq, lam_init):
    nq = seq // tq
    const = lambda b, h, i: (0, 0)
    return pl.pallas_call(
        functools.partial(_df_prompt_kernel, tq=tq, lam_init=lam_init),
        grid=(batch, DF_HEADS, nq),
        in_specs=[pl.BlockSpec(lam_p.shape, const), pl.BlockSpec(g_subln.shape, const),
                  pl.BlockSpec((tq, LANES), lambda b, h, i: (b * nq + i, h)),
                  pl.BlockSpec((seq, LANES), lambda b, h, i: (b, h)),
                  pl.BlockSpec((seq, LANES), lambda b, h, i: (b, DF_HEADS + h))],
        out_specs=pl.BlockSpec((tq, LANES), lambda b, h, i: (b * nq + i, h)),
        out_shape=jax.ShapeDtypeStruct((batch * seq, DF_WIDTH), BF16),
        compiler_params=_cparams(("parallel", "parallel", "arbitrary")),
        name="df_prompt",
    )(lam_p, g_subln, q, kv16, kv16)


def _sample_kernel(pt_ref, lam_ref, g_ref, later_ref, qsb_ref, qdf_ref, nsb_ref, ndf_ref, *rest,
                   pages_per_step, page, dec, lam_init):
    sb_pages = rest[:pages_per_step]
    df_pages = rest[pages_per_step:2 * pages_per_step]
    osb_ref, odf_ref = rest[2 * pages_per_step:2 * pages_per_step + 2]
    acc_sb, c_sb, acc_df, m_df, l_df = rest[2 * pages_per_step + 2:]
    j = pl.program_id(1)
    n_sb = SB_HEADS * dec
    n_df = DF_HEADS * 2 * dec
    lane = lax.broadcasted_iota(jnp.int32, (dec, LANES), 1)

    qrow = lax.broadcasted_iota(jnp.int32, (n_sb, SB_WIDTH), 0) // dec
    qcol = lax.broadcasted_iota(jnp.int32, (n_sb, SB_WIDTH), 1) // HEAD_DIM
    q_sb = jnp.where(qrow == qcol, jnp.concatenate([qsb_ref[...]] * SB_HEADS, axis=0), 0.0).astype(BF16)
    q_df = qdf_ref[...]
    q_df_heads = []
    for h in range(DF_HEADS):
        qh = q_df[:, h * LANES:(h + 1) * LANES]
        q_df_heads.append(jnp.concatenate(
            [jnp.where(lane < HEAD_DIM, qh, 0.0), jnp.where(lane >= HEAD_DIM, qh, 0.0)], axis=0).astype(BF16))

    def pad_tokens(t):
        if t.shape[0] == page:
            return t
        return jnp.concatenate([t, jnp.zeros((page - t.shape[0], t.shape[1]), t.dtype)], axis=0)

    def sb_weights(z, masked):
        soft = jnp.log(1.0 + jnp.exp(-jnp.abs(z)))
        log_beta = jnp.minimum(z, 0.0) - soft
        log_1mb = log_beta - z
        if masked:
            rows = lax.broadcasted_iota(jnp.int32, (n_sb, LANES), 0) % dec
            cols = lax.broadcasted_iota(jnp.int32, (n_sb, LANES), 1)
            valid = cols < rows
            log_1mb = jnp.where(valid, log_1mb, 0.0)
        hi = log_1mb.astype(BF16)
        lo = (log_1mb - hi.astype(F32)).astype(BF16)
        both = _dot(jnp.concatenate([hi, lo], axis=0), later_ref[...])
        rem = both[:n_sb, :page] + both[n_sb:, :page]
        total = both[:n_sb, page:] + both[n_sb:, page:]
        w = jnp.exp(log_beta + rem + c_sb[...])
        if masked:
            w = jnp.where(valid, w, 0.0)
        c_sb[...] += total
        return w.astype(BF16)

    def sb_page(ref):
        w = sb_weights(_dot(q_sb, ref[:SB_WIDTH, :].astype(BF16)), False)
        acc_sb[...] += _dot_nt(w, ref[SB_WIDTH:, :].astype(BF16))

    def sb_new(ref):
        w = sb_weights(_dot_nt(q_sb, pad_tokens(ref[:, :SB_WIDTH]).astype(BF16)), True)
        acc_sb[...] += _dot(w, pad_tokens(ref[:, SB_WIDTH:]).astype(BF16))

    def df_block(ref, ntok, masked):
        def fetch(r):
            return pad_tokens(ref[pl.ds(r, ntok, stride=2 * DF_HEADS), :]).astype(BF16)
        s = jnp.concatenate([_dot_nt(q_df_heads[h], fetch(h)) for h in range(DF_HEADS)], axis=0)
        if masked:
            rows = lax.broadcasted_iota(jnp.int32, (n_df, LANES), 0) % dec
            cols = lax.broadcasted_iota(jnp.int32, (n_df, LANES), 1)
            s = jnp.where(cols <= rows, s, NEG_BIG)
        m_old = m_df[...]
        m_new = jnp.maximum(m_old, jnp.max(s, axis=1, keepdims=True))
        alpha = jnp.exp(m_old - m_new)
        p = jnp.exp(s - m_new)
        l_df[...] = l_df[...] * alpha + jnp.sum(p, axis=1, keepdims=True)
        pv = [_dot(p[h * 2 * dec:(h + 1) * 2 * dec].astype(BF16), fetch(DF_HEADS + h))
              for h in range(DF_HEADS)]
        acc_df[...] = acc_df[...] * alpha + jnp.concatenate(pv, axis=0)
        m_df[...] = m_new

    @pl.when(j == 0)
    def _():
        acc_sb[...] = jnp.zeros_like(acc_sb)
        c_sb[...] = jnp.zeros_like(c_sb)
        acc_df[...] = jnp.zeros_like(acc_df)
        m_df[...] = jnp.full_like(m_df, NEG_BIG)
        l_df[...] = jnp.zeros_like(l_df)
        sb_new(nsb_ref)
        df_block(ndf_ref, dec, True)

    for r in range(pages_per_step):
        sb_page(sb_pages[r])
        df_block(df_pages[r], page, False)

    @pl.when(j == pl.num_programs(1) - 1)
    def _():
        a = acc_sb[...]
        osb_ref[...] = jnp.concatenate(
            [a[h * dec:(h + 1) * dec, h * HEAD_DIM:(h + 1) * HEAD_DIM] for h in range(SB_HEADS)],
            axis=1).astype(osb_ref.dtype)
        d = acc_df[...] / l_df[...]
        lam = _lambda(lam_ref, lam_init)
        outs = []
        for h in range(DF_HEADS):
            o = d[h * 2 * dec:h * 2 * dec + dec] - lam * d[h * 2 * dec + dec:(h + 1) * 2 * dec]
            outs.append(_rms(o, g_ref[...]) * (1.0 - lam_init))
        odf_ref[...] = jnp.concatenate(outs, axis=1).astype(odf_ref.dtype)


def _sample_attention(page_table, lam_p, g_subln, q_sb, q_df, new_sb, new_df, cache_sb, cache_df,
                      lam_init, pages_per_step):
    n_seq, n_pages = page_table.shape
    dec = q_sb.shape[0] // n_seq
    n_pool, _, page = cache_sb.shape[:3]
    sb_rows = 2 * SB_WIDTH
    df_rows = page * 2 * DF_HEADS
    csb = jnp.transpose(cache_sb, (0, 1, 3, 4, 5, 2)).reshape(n_pool, sb_rows, page)
    cdf = cache_df.reshape(n_pool, df_rows, 2 * HEAD_DIM)
    ndf = new_df.reshape(n_seq, dec * 2 * DF_HEADS, 2 * HEAD_DIM)
    jj = lax.broadcasted_iota(jnp.int32, (page, 2 * page), 0)
    ss = lax.broadcasted_iota(jnp.int32, (page, 2 * page), 1)
    later = ((jj > ss) | (ss >= page)).astype(BF16)
    n_steps = n_pages // pages_per_step
    const = lambda s, j, pt: (0, 0)
    seq_blk = lambda s, j, pt: (s, 0)

    def page_map(r):
        return lambda s, j, pt: (pt[s, n_pages - 1 - (j * pages_per_step + r)], 0, 0)

    in_specs = [pl.BlockSpec(lam_p.shape, const), pl.BlockSpec(g_subln.shape, const),
                pl.BlockSpec(later.shape, const),
                pl.BlockSpec((dec, SB_WIDTH), seq_blk), pl.BlockSpec((dec, DF_WIDTH), seq_blk),
                pl.BlockSpec((dec, 2 * SB_WIDTH), seq_blk),
                pl.BlockSpec((None,) + ndf.shape[1:], lambda s, j, pt: (s, 0, 0))]
    in_specs += [pl.BlockSpec((None, sb_rows, page), page_map(r)) for r in range(pages_per_step)]
    in_specs += [pl.BlockSpec((None, df_rows, 2 * HEAD_DIM), page_map(r)) for r in range(pages_per_step)]
    n_sb, n_df = SB_HEADS * dec, DF_HEADS * 2 * dec
    grid_spec = pltpu.PrefetchScalarGridSpec(
        num_scalar_prefetch=1,
        grid=(n_seq, n_steps),
        in_specs=in_specs,
        out_specs=(pl.BlockSpec((dec, SB_WIDTH), seq_blk), pl.BlockSpec((dec, DF_WIDTH), seq_blk)),
        scratch_shapes=[pltpu.VMEM((n_sb, SB_WIDTH), F32), pltpu.VMEM((n_sb, LANES), F32),
                        pltpu.VMEM((n_df, LANES), F32), pltpu.VMEM((n_df, 1), F32),
                        pltpu.VMEM((n_df, 1), F32)])
    return pl.pallas_call(
        functools.partial(_sample_kernel, pages_per_step=pages_per_step, page=page, dec=dec,
                          lam_init=lam_init),
        grid_spec=grid_spec,
        out_shape=(jax.ShapeDtypeStruct((n_seq * dec, SB_WIDTH), F32),
                   jax.ShapeDtypeStruct((n_seq * dec, DF_WIDTH), F32)),
        compiler_params=_cparams(("parallel", "arbitrary")),
        name="sample_attn",
    )(page_table, lam_p, g_subln, later, q_sb, q_df, new_sb, ndf,
      *([csb] * pages_per_step), *([cdf] * pages_per_step))


def _route(logits):
    lane = lax.broadcasted_iota(jnp.int32, logits.shape, 1).astype(F32)
    big = 1e9

    def first_argmax(vals, valid):
        v = jnp.where(valid, vals, -jnp.inf)
        top = jnp.max(v, axis=1, keepdims=True)
        idx = jnp.min(jnp.where(valid & (v == top), lane, big), axis=1, keepdims=True)
        return top, idx

    is_grp = lane < N_GROUPS
    g_top, g_idx = first_argmax(logits, is_grp)
    p_g = 1.0 / jnp.sum(jnp.where(is_grp, jnp.exp(logits - g_top), 0.0), axis=1, keepdims=True)
    lo = N_GROUPS + g_idx * EXPERTS_PER_GROUP
    in_grp = (lane >= lo) & (lane < lo + EXPERTS_PER_GROUP)
    v1, i1 = first_argmax(logits, in_grp)
    v2, i2 = first_argmax(logits, in_grp & (lane != i1))
    e2 = jnp.exp(v2 - v1)
    w1 = p_g / (1.0 + e2)
    w2 = p_g * e2 / (1.0 + e2)
    comb = jnp.where(lane == i1, w1, 0.0) + jnp.where(lane == i2, w2, 0.0)
    return pltpu.roll(comb, LANES - N_GROUPS, 1)


def _merge_kernel(x_ref, osb_ref, odf_ref, gate_ref, wsb_ref, wdf_ref, wo_ref, g_ref, wr_ref,
                  xm_ref, t_ref, comb_ref):
    a = _dot(osb_ref[...].astype(BF16), wsb_ref[...])
    b = _dot(odf_ref[...].astype(BF16), wdf_ref[...])
    merged = gate_ref[:, :D_MODEL] * a + gate_ref[:, D_MODEL:] * b
    xm = x_ref[...] + _dot(merged.astype(BF16), wo_ref[...])
    xm_ref[...] = xm
    t = _rms(xm, g_ref[...])
    t_ref[...] = t.astype(BF16)
    logits = jnp.dot(t, wr_ref[...], preferred_element_type=F32, precision=lax.Precision.HIGHEST)
    comb_ref[...] = _route(logits)


def _merge(x, o_sb, o_df, gates, w_sb16, w_df16, w_o16, g_ffn, w_route, tm):
    n = x.shape[0]
    row = lambda i: (i, 0)
    const = lambda i: (0, 0)
    return pl.pallas_call(
        _merge_kernel,
        grid=(n // tm,),
        in_specs=[pl.BlockSpec((tm, D_MODEL), row), pl.BlockSpec((tm, SB_WIDTH), row),
                  pl.BlockSpec((tm, DF_WIDTH), row), pl.BlockSpec((tm, 2 * D_MODEL), row),
                  pl.BlockSpec(w_sb16.shape, const), pl.BlockSpec(w_df16.shape, const),
                  pl.BlockSpec(w_o16.shape, const), pl.BlockSpec((1, D_MODEL), const),
                  pl.BlockSpec(w_route.shape, const)],
        out_specs=(pl.BlockSpec((tm, D_MODEL), row), pl.BlockSpec((tm, D_MODEL), row),
                   pl.BlockSpec((tm, LANES), row)),
        out_shape=(jax.ShapeDtypeStruct((n, D_MODEL), F32), jax.ShapeDtypeStruct((n, D_MODEL), BF16),
                   jax.ShapeDtypeStruct((n, LANES), F32)),
        compiler_params=_cparams(("parallel",)),
        name="merge_route",
    )(x, o_sb, o_df, gates, w_sb16, w_df16, w_o16, g_ffn, w_route)


def _moe_kernel(t_ref, comb_ref, xm_ref, wg_ref, wu_ref, wd_ref, o_ref):
    g = pl.program_id(1)

    @pl.when(g == 0)
    def _():
        o_ref[...] = xm_ref[...]

    t = t_ref[...]
    comb = comb_ref[...]
    lane = lax.broadcasted_iota(jnp.int32, comb.shape, 1)
    acc = jnp.zeros(o_ref.shape, F32)
    for e in range(EXPERTS_PER_GROUP):
        gate = _dot(t, wg_ref[e])
        up = _dot(t, wu_ref[e])
        hid = gate * jax.nn.sigmoid(gate) * up
        c = jnp.sum(jnp.where(lane == g * EXPERTS_PER_GROUP + e, comb, 0.0), axis=1, keepdims=True)
        acc = acc + c * _dot(hid.astype(BF16), wd_ref[e])
    o_ref[...] += acc


def _moe(t, comb, xm, w_gate16, w_up16, w_down16, tm):
    n = t.shape[0]
    row = lambda i, g: (i, 0)
    grp = lambda i, g: (g, 0, 0)
    return pl.pallas_call(
        _moe_kernel,
        grid=(n // tm, N_GROUPS),
        in_specs=[pl.BlockSpec((tm, D_MODEL), row), pl.BlockSpec((tm, LANES), row),
                  pl.BlockSpec((tm, D_MODEL), row),
                  pl.BlockSpec((EXPERTS_PER_GROUP, D_MODEL, D_EXPERT), grp),
                  pl.BlockSpec((EXPERTS_PER_GROUP, D_MODEL, D_EXPERT), grp),
                  pl.BlockSpec((EXPERTS_PER_GROUP, D_EXPERT, D_MODEL), grp)],
        out_specs=pl.BlockSpec((tm, D_MODEL), row),
        out_shape=jax.ShapeDtypeStruct((n, D_MODEL), F32),
        compiler_params=_cparams(("parallel", "arbitrary")),
        name="moe",
    )(t, comb, xm, w_gate16, w_up16, w_down16)


def _ple_kernel(x_ref, p_ref, gp_ref, wg_ref, wp_ref, gf_ref, y_ref):
    x = x_ref[...]
    gate = jax.nn.sigmoid(_dot(_rms(x, gp_ref[...]).astype(BF16), wg_ref[...]))
    x = x + gate * _dot(p_ref[...].astype(BF16), wp_ref[...])
    y_ref[...] = _rms(x, gf_ref[...])


def _ple(x, p, g_ple, w_gate16, w_proj16, g_final, tm):
    n = x.shape[0]
    row = lambda i: (i, 0)
    const = lambda i: (0, 0)
    return pl.pallas_call(
        _ple_kernel,
        grid=(n // tm,),
        in_specs=[pl.BlockSpec((tm, D_MODEL), row), pl.BlockSpec((tm, p.shape[1]), row),
                  pl.BlockSpec((1, D_MODEL), const), pl.BlockSpec(w_gate16.shape, const),
                  pl.BlockSpec(w_proj16.shape, const), pl.BlockSpec((1, D_MODEL), const)],
        out_specs=pl.BlockSpec((tm, D_MODEL), row),
        out_shape=jax.ShapeDtypeStruct((n, D_MODEL), F32),
        compiler_params=_cparams(("parallel",)),
        name="ple_final",
    )(x, p, g_ple, w_gate16, w_proj16, g_final)


def _pick_tile(n, prefer):
    for t in prefer:
        if n % t == 0:
            return t
    raise ValueError(f"no row tile for {n} rows")


def kernel(x_prompt, x_sample, p_prompt, p_sample, cache_sb, cache_df, page_table, g_mix, w_in, lam_q1, lam_k1, lam_q2, lam_k2, g_subln, w_sb, w_df, w_o, g_ffn, w_grp, w_rtr, w_gate, w_up, w_down, g_ple, w_ple_gate, w_ple_proj, g_final):
    depth = g_mix.shape[0]
    assert depth == 1, "single-layer step only"
    batch, seq, _ = x_prompt.shape
    n_seq, dec, _ = x_sample.shape
    n_pages = page_table.shape[1]
    page = cache_sb.shape[2]
    past = n_pages * page
    lam_init = 0.8 - 0.6 * math.exp(-0.3 * 0)

    w_in16 = w_in[0].astype(BF16)
    w_sb16, w_df16, w_o16 = w_sb[0].astype(BF16), w_df[0].astype(BF16), w_o[0].astype(BF16)
    w_gate16, w_up16, w_down16 = w_gate[0].astype(BF16), w_up[0].astype(BF16), w_down[0].astype(BF16)
    w_pg16, w_pp16 = w_ple_gate[0].astype(BF16), w_ple_proj[0].astype(BF16)
    w_route = jnp.concatenate(
        [w_grp[0], jnp.transpose(w_rtr[0], (1, 0, 2)).reshape(D_MODEL, N_EXPERTS),
         jnp.zeros((D_MODEL, LANES - N_GROUPS - N_EXPERTS), F32)], axis=1)
    lam_p = jnp.concatenate([lam_q1, lam_k1, lam_q2, lam_k2], axis=0).astype(F32)
    g_final2 = g_final.reshape(1, D_MODEL)

    def finish(x, o_sb, o_df, gates, p):
        tm = _pick_tile(x.shape[0], (512, 256, 128, 64, 32, 16, 8))
        xm, t, comb = _merge(x, o_sb, o_df, gates, w_sb16, w_df16, w_o16, g_ffn, w_route, tm)
        x2 = _moe(t, comb, xm, w_gate16, w_up16, w_down16, tm)
        return _ple(x2, p, g_ple, w_pg16, w_pp16, g_final2, tm)

    n_p = batch * seq
    tm_p = _pick_tile(seq, (256, 128))
    tq = _pick_tile(seq, (256, 128))
    xp = x_prompt.reshape(n_p, D_MODEL)
    tabs_p = _rope_tables(jnp.arange(seq, dtype=jnp.int32))
    q_sb, q_df, new_sb_p, new_df_p, sb16, df16, gates = _project(
        xp, g_mix, w_in16, tabs_p, seq // tm_p, BF16, tm_p)
    o_sb = _sb_prompt(q_sb, sb16, batch, seq, tq)
    o_df = _df_prompt(lam_p, g_subln, q_df, df16, batch, seq, tq, lam_init)
    y_prompt = finish(xp, o_sb, o_df, gates, p_prompt[0].reshape(n_p, -1)).reshape(x_prompt.shape)

    n_s = n_seq * dec
    tm_s = _pick_tile(n_s, (256, 128, 64, 32, 16, 8))
    xs = x_sample.reshape(n_s, D_MODEL)
    pos_s = past + jnp.arange(dec, dtype=jnp.int32)
    tabs_s = tuple(jnp.tile(t, (tm_s // dec, 1)) for t in _rope_tables(pos_s))
    q_sb, q_df, new_sb_s, new_df_s, _, _, gates = _project(xs, g_mix, w_in16, tabs_s, 1, F32, tm_s)
    pages_per_step = 4 if n_pages % 4 == 0 else 1
    o_sb, o_df = _sample_attention(page_table, lam_p, g_subln, q_sb, q_df, new_sb_s, new_df_s,
                                   cache_sb, cache_df, lam_init, pages_per_step)
    y_sample = finish(xs, o_sb, o_df, gates, p_sample[0].reshape(n_s, -1)).reshape(x_sample.shape)

    return (y_prompt, y_sample,
            new_sb_p.reshape(batch, 1, seq, 2, SB_HEADS, HEAD_DIM),
            new_df_p.reshape(batch, 1, seq, 2, DF_HEADS, 2 * HEAD_DIM),
            new_sb_s.reshape(n_seq, 1, dec, 2, SB_HEADS, HEAD_DIM),
            new_df_s.reshape(n_seq, 1, dec, 2, DF_HEADS, 2 * HEAD_DIM))
```

```python
import functools
import math

import jax
import jax.numpy as jnp
from jax import lax
from jax.experimental import pallas as pl
from jax.experimental.pallas import tpu as pltpu

F32 = jnp.float32
BF16 = jnp.bfloat16

RMS_EPS = 1e-6
ROPE_THETA = 10000.0
D_MODEL = 1024
SB_HEADS = 8
DF_HEADS = 4
HEAD_DIM = 64
SB_WIDTH = SB_HEADS * HEAD_DIM
DF_WIDTH = DF_HEADS * 2 * HEAD_DIM
N_GROUPS = 4
EXPERTS_PER_GROUP = 8
N_EXPERTS = N_GROUPS * EXPERTS_PER_GROUP
D_EXPERT = 256
LANES = 128
NEG_BIG = -1e30
SCORE_SCALE = HEAD_DIM ** -0.5 * math.log2(math.e)
VMEM_LIMIT = 56 * 1024 * 1024


def _cparams(sem):
    return pltpu.CompilerParams(dimension_semantics=sem, vmem_limit_bytes=VMEM_LIMIT)


def _rms(x, g):
    ms = jnp.mean(x * x, axis=-1, keepdims=True)
    return x * lax.rsqrt(ms + RMS_EPS) * g


def _dot(a, b):
    return jnp.dot(a, b, preferred_element_type=F32)


def _dot_nt(a, b):
    return lax.dot_general(a, b, (((1,), (1,)), ((), ())), preferred_element_type=F32)


def _proj_kernel(x_ref, g_ref, w_ref, cos_ref, sa_ref, sb_ref,
                 qsb_ref, qdf_ref, nsb_ref, ndf_ref, sb16_ref, df16_ref, gate_ref, *, tokens_on_lanes):
    tm = x_ref.shape[0]
    h = _rms(x_ref[...], g_ref[...]).astype(BF16)

    def put_sb(part, t):
        if tokens_on_lanes:
            nsb_ref[part * SB_WIDTH:(part + 1) * SB_WIDTH, :] = t.T
        else:
            nsb_ref[:, part * SB_WIDTH:(part + 1) * SB_WIDTH] = t
        sb16_ref[:, part * SB_WIDTH:(part + 1) * SB_WIDTH] = t.astype(BF16)

    def put_df(part, t):
        for hd in range(DF_HEADS):
            ndf_ref[pl.ds(part * DF_HEADS + hd, tm, stride=2 * DF_HEADS), :] = t[:, hd * LANES:(hd + 1) * LANES]
        df16_ref[:, part * DF_WIDTH:(part + 1) * DF_WIDTH] = t.astype(BF16)

    cos, sa, sb = cos_ref[...], sa_ref[...], sb_ref[...]

    def mm(c):
        return _dot(h, w_ref[:, c * 1024:(c + 1) * 1024])

    def rope(t):
        parts = []
        for c in range(t.shape[1] // LANES):
            u = t[:, c * LANES:(c + 1) * LANES]
            parts.append(u * cos + pltpu.roll(u, 96, 1) * sa + pltpu.roll(u, 32, 1) * sb)
        return jnp.concatenate(parts, axis=1)

    z = mm(0)
    qsb_ref[...] = (z[:, :512] * SCORE_SCALE).astype(qsb_ref.dtype)
    put_sb(0, z[:, 512:])
    z = mm(1)
    put_sb(1, z[:, :512])
    qdf_ref[...] = (rope(z[:, 512:]) * SCORE_SCALE).astype(qdf_ref.dtype)
    z = mm(2)
    put_df(0, rope(z[:, :512]))
    put_df(1, z[:, 512:])
    gate_ref[:, :1024] = jax.nn.sigmoid(mm(3))
    gate_ref[:, 1024:] = jax.nn.sigmoid(mm(4))


def _project(x, g, w_in16, tables, period_blocks, q_dtype, tm, tokens_on_lanes_seq=None):
    n = x.shape[0]
    cos, sa, sb = tables
    row = lambda i: (i, 0)
    const = lambda i: (0, 0)
    tab = lambda i: (i % period_blocks, 0)
    if tokens_on_lanes_seq is None:
        nsb_shape, nsb_spec = (n, 2 * SB_WIDTH), pl.BlockSpec((tm, 2 * SB_WIDTH), row)
    else:
        per_seq = tokens_on_lanes_seq // tm
        nsb_shape = (n // tokens_on_lanes_seq, 2 * SB_WIDTH, tokens_on_lanes_seq)
        nsb_spec = pl.BlockSpec((None, 2 * SB_WIDTH, tm), lambda i: (i // per_seq, 0, i % per_seq))
    rows_df = 2 * DF_HEADS
    out_shape = (
        jax.ShapeDtypeStruct((n, 512), q_dtype), jax.ShapeDtypeStruct((n, 512), q_dtype),
        jax.ShapeDtypeStruct(nsb_shape, F32), jax.ShapeDtypeStruct((n * rows_df, 2 * HEAD_DIM), F32),
        jax.ShapeDtypeStruct((n, 1024), BF16), jax.ShapeDtypeStruct((n, 1024), BF16),
        jax.ShapeDtypeStruct((n, 2048), F32))
    return pl.pallas_call(
        functools.partial(_proj_kernel, tokens_on_lanes=tokens_on_lanes_seq is not None),
        grid=(n // tm,),
        in_specs=[pl.BlockSpec((tm, D_MODEL), row), pl.BlockSpec((1, D_MODEL), const),
                  pl.BlockSpec(w_in16.shape, const, pipeline_mode=pl.Buffered(1)),
                  pl.BlockSpec((tm, LANES), tab), pl.BlockSpec((tm, LANES), tab),
                  pl.BlockSpec((tm, LANES), tab)],
        out_specs=(pl.BlockSpec((tm, 512), row), pl.BlockSpec((tm, 512), row),
                   nsb_spec, pl.BlockSpec((tm * rows_df, 2 * HEAD_DIM), row),
                   pl.BlockSpec((tm, 1024), row), pl.BlockSpec((tm, 1024), row),
                   pl.BlockSpec((tm, 2048), row)),
        out_shape=out_shape,
        compiler_params=_cparams(("parallel",)),
        name="proj",
    )(x, g, w_in16, cos, sa, sb)


def _rope_tables(pos):
    half = HEAD_DIM // 2
    inv = 1.0 / (ROPE_THETA ** (jnp.arange(half, dtype=F32) * 2.0 / HEAD_DIM))
    ang = pos.astype(F32)[:, None] * inv[None, :]
    cos, sin = jnp.cos(ang), jnp.sin(ang)
    zero = jnp.zeros_like(sin)
    cos64 = jnp.concatenate([cos, cos], axis=1)
    sa64 = jnp.concatenate([-sin, zero], axis=1)
    sb64 = jnp.concatenate([zero, sin], axis=1)
    rep = lambda t: jnp.concatenate([t, t], axis=1)
    return rep(cos64), rep(sa64), rep(sb64)


def _sb_prompt_kernel(q_ref, k_ref, v_ref, o_ref, *, tq, cols):
    i = pl.program_id(2)
    lane = lax.broadcasted_iota(jnp.int32, (tq, LANES), 1)
    row = lax.broadcasted_iota(jnp.int32, (tq, tq), 0)
    col = lax.broadcasted_iota(jnp.int32, (tq, tq), 1)
    later = (row > col).astype(BF16)
    causal = col < row
    heads = []
    for cb in range(cols):
        q = q_ref[:, cb * LANES:(cb + 1) * LANES].astype(F32)
        heads.append((cb, jnp.where(lane < HEAD_DIM, q, 0.0).astype(BF16)))
        heads.append((cb, jnp.where(lane >= HEAD_DIM, q, 0.0).astype(BF16)))

    def tile(kb, state, masked):
        start = pl.multiple_of(kb * tq, tq)
        new = []
        for (cb, qh), (c, acc) in zip(heads, state):
            k = k_ref[pl.ds(start, tq), cb * LANES:(cb + 1) * LANES]
            v = v_ref[pl.ds(start, tq), cb * LANES:(cb + 1) * LANES]
            z = _dot_nt(qh, k)
            soft = jnp.log2(1.0 + jnp.exp2(-jnp.abs(z)))
            log_beta = jnp.minimum(z, 0.0) - soft
            log_1mb = log_beta - z
            if masked:
                log_1mb = jnp.where(causal, log_1mb, 0.0)
            rem = _dot(log_1mb.astype(BF16), later)
            w = jnp.exp2(log_beta + rem + c)
            if masked:
                w = jnp.where(causal, w, 0.0)
            acc = acc + _dot(w.astype(BF16), v)
            c = c + jnp.sum(log_1mb, axis=1, keepdims=True)
            new.append((c, acc))
        return tuple(new)

    init = tuple((jnp.zeros((tq, 1), F32), jnp.zeros((tq, LANES), F32)) for _ in heads)
    state = tile(i, init, True)
    state = lax.fori_loop(0, i, lambda s, st: tile(i - 1 - s, st, False), state)
    for cb in range(cols):
        o_ref[:, cb * LANES:(cb + 1) * LANES] = jnp.where(
            lane < HEAD_DIM, state[2 * cb][1], state[2 * cb + 1][1]).astype(o_ref.dtype)


def _sb_prompt(q, kv16, batch, seq, tq, cols):
    nq = seq // tq
    width = cols * LANES
    n_blk = SB_WIDTH // width
    return pl.pallas_call(
        functools.partial(_sb_prompt_kernel, tq=tq, cols=cols),
        grid=(batch, n_blk, nq),
        in_specs=[pl.BlockSpec((tq, width), lambda b, p, i: (b * nq + i, p)),
                  pl.BlockSpec((seq, width), lambda b, p, i: (b, p)),
                  pl.BlockSpec((seq, width), lambda b, p, i: (b, n_blk + p))],
        out_specs=pl.BlockSpec((tq, width), lambda b, p, i: (b * nq + i, p)),
        out_shape=jax.ShapeDtypeStruct((batch * seq, SB_WIDTH), BF16),
        compiler_params=_cparams(("parallel", "parallel", "arbitrary")),
        name="sb_prompt",
    )(q, kv16, kv16)


def _lambda(lam_ref, lam_init):
    lp = lam_ref[...]
    a = jnp.sum(lp[0:1] * lp[1:2], axis=1, keepdims=True)
    b = jnp.sum(lp[2:3] * lp[3:4], axis=1, keepdims=True)
    return jnp.exp(a) - jnp.exp(b) + lam_init


def _df_prompt_kernel(lam_ref, g_ref, q_ref, k_ref, v_ref, o_ref, *, tq, cols, lam_init):
    i = pl.program_id(2)
    lane = lax.broadcasted_iota(jnp.int32, (tq, LANES), 1)
    row = lax.broadcasted_iota(jnp.int32, (tq, tq), 0)
    col = lax.broadcasted_iota(jnp.int32, (tq, tq), 1)
    causal = col <= row
    halves = []
    for cb in range(cols):
        q = q_ref[:, cb * LANES:(cb + 1) * LANES].astype(F32)
        halves.append((cb, jnp.where(lane < HEAD_DIM, q, 0.0).astype(BF16)))
        halves.append((cb, jnp.where(lane >= HEAD_DIM, q, 0.0).astype(BF16)))

    ones = jnp.ones((tq, LANES), BF16)

    def tile(kb, st, masked):
        start = pl.multiple_of(kb * tq, tq)
        new = []
        for (cb, qx), (m, acc) in zip(halves, st):
            k = k_ref[pl.ds(start, tq), cb * LANES:(cb + 1) * LANES]
            v = v_ref[pl.ds(start, tq), cb * LANES:(cb + 1) * LANES]
            s = _dot_nt(qx, k)
            if masked:
                s = jnp.where(causal, s, NEG_BIG)
            m_new = jnp.maximum(m, jnp.max(s, axis=1, keepdims=True))
            alpha = jnp.exp2(m - m_new)
            p = jnp.exp2(s - m_new)
            acc = acc * alpha + _dot(p.astype(BF16), jnp.concatenate([v, ones], axis=1))
            new.append((m_new, acc))
        return tuple(new)

    init = (jnp.full((tq, 1), NEG_BIG, F32), jnp.zeros((tq, 2 * LANES), F32))
    st = tile(i, (init,) * len(halves), True)
    st = lax.fori_loop(0, i, lambda s, c: tile(i - 1 - s, c, False), st)
    lam = _lambda(lam_ref, lam_init)
    for cb in range(cols):
        (_, e1), (_, e2) = st[2 * cb], st[2 * cb + 1]
        o = e1[:, :LANES] / e1[:, LANES:] - lam * (e2[:, :LANES] / e2[:, LANES:])
        o_ref[:, cb * LANES:(cb + 1) * LANES] = (_rms(o, g_ref[...]) * (1.0 - lam_init)).astype(o_ref.dtype)


def _df_prompt(lam_p, g_subln, q, kv16, batch, seq, tq, cols, lam_init):
    nq = seq // tq
    width = cols * LANES
    n_blk = DF_WIDTH // width
    const = lambda b, h, i: (0, 0)
    return pl.pallas_call(
        functools.partial(_df_prompt_kernel, tq=tq, cols=cols, lam_init=lam_init),
        grid=(batch, n_blk, nq),
        in_specs=[pl.BlockSpec(lam_p.shape, const), pl.BlockSpec(g_subln.shape, const),
                  pl.BlockSpec((tq, width), lambda b, h, i: (b * nq + i, h)),
                  pl.BlockSpec((seq, width), lambda b, h, i: (b, h)),
                  pl.BlockSpec((seq, width), lambda b, h, i: (b, n_blk + h))],
        out_specs=pl.BlockSpec((tq, width), lambda b, h, i: (b * nq + i, h)),
        out_shape=jax.ShapeDtypeStruct((batch * seq, DF_WIDTH), BF16),
        compiler_params=_cparams(("parallel", "parallel", "arbitrary")),
        name="df_prompt",
    )(lam_p, g_subln, q, kv16, kv16)


def _sample_kernel(pt_ref, lam_ref, g_ref, later_ref, qsb_ref, qdf_ref, nsb_ref, ndf_ref, *rest,
                   pages_per_step, page, dec, lam_init):
    sb_pages = rest[:pages_per_step]
    df_pages = rest[pages_per_step:2 * pages_per_step]
    osb_ref, odf_ref = rest[2 * pages_per_step:2 * pages_per_step + 2]
    acc_sb, c_sb, acc_df, m_df, l_df = rest[2 * pages_per_step + 2:]
    j = pl.program_id(1)
    n_sb = SB_HEADS * dec
    n_df = DF_HEADS * 2 * dec
    lane = lax.broadcasted_iota(jnp.int32, (dec, LANES), 1)

    qrow = lax.broadcasted_iota(jnp.int32, (n_sb, SB_WIDTH), 0) // dec
    qcol = lax.broadcasted_iota(jnp.int32, (n_sb, SB_WIDTH), 1) // HEAD_DIM
    q_sb = jnp.where(qrow == qcol, jnp.concatenate([qsb_ref[...]] * SB_HEADS, axis=0), 0.0).astype(BF16)
    q_df = qdf_ref[...]
    q_df_heads = []
    for h in range(DF_HEADS):
        qh = q_df[:, h * LANES:(h + 1) * LANES]
        q_df_heads.append(jnp.concatenate(
            [jnp.where(lane < HEAD_DIM, qh, 0.0), jnp.where(lane >= HEAD_DIM, qh, 0.0)], axis=0).astype(BF16))

    def pad_tokens(t):
        if t.shape[0] == page:
            return t
        return jnp.concatenate([t, jnp.zeros((page - t.shape[0], t.shape[1]), t.dtype)], axis=0)

    def sb_weights(z, masked):
        n = z.shape[1] // page
        soft = jnp.log2(1.0 + jnp.exp2(-jnp.abs(z)))
        log_beta = jnp.minimum(z, 0.0) - soft
        log_1mb = log_beta - z
        if masked:
            rows = lax.broadcasted_iota(jnp.int32, z.shape, 0) % dec
            cols = lax.broadcasted_iota(jnp.int32, z.shape, 1)
            valid = cols < rows
            log_1mb = jnp.where(valid, log_1mb, 0.0)
        hi = log_1mb.astype(BF16).astype(F32)
        parts = [t[:, r * page:(r + 1) * page] for t in (hi, log_1mb - hi) for r in range(n)]
        both = _dot(jnp.concatenate(parts, axis=0).astype(BF16), later_ref[...])
        c = c_sb[...]
        ws = []
        for r in range(n):
            top = both[r * n_sb:(r + 1) * n_sb] + both[(n + r) * n_sb:(n + r + 1) * n_sb]
            ws.append(jnp.exp2(log_beta[:, r * page:(r + 1) * page] + top[:, :page] + c))
            c = c + top[:, page:]
        c_sb[...] = c
        w = ws[0] if n == 1 else jnp.concatenate(ws, axis=1)
        if masked:
            w = jnp.where(valid, w, 0.0)
        return w.astype(BF16)

    def sb_pages_step(refs):
        kt = jnp.concatenate([ref[:SB_WIDTH, :].astype(BF16) for ref in refs], axis=1)
        w = sb_weights(_dot(q_sb, kt), False)
        vt = jnp.concatenate([ref[SB_WIDTH:, :].astype(BF16) for ref in refs], axis=1)
        acc_sb[...] += _dot_nt(w, vt)

    def sb_new(ref):
        w = sb_weights(_dot_nt(q_sb, pad_tokens(ref[:, :SB_WIDTH]).astype(BF16)), True)
        acc_sb[...] += _dot(w, pad_tokens(ref[:, SB_WIDTH:]).astype(BF16))

    def df_block(refs, ntok, masked):
        def fetch(r):
            rows = [pad_tokens(ref[pl.ds(r, ntok, stride=2 * DF_HEADS), :]).astype(BF16) for ref in refs]
            return rows[0] if len(rows) == 1 else jnp.concatenate(rows, axis=0)
        s = jnp.concatenate([_dot_nt(q_df_heads[h], fetch(h)) for h in range(DF_HEADS)], axis=0)
        if masked:
            rows = lax.broadcasted_iota(jnp.int32, s.shape, 0) % dec
            cols = lax.broadcasted_iota(jnp.int32, s.shape, 1)
            s = jnp.where(cols <= rows, s, NEG_BIG)
        m_old = m_df[...]
        m_new = jnp.maximum(m_old, jnp.max(s, axis=1, keepdims=True))
        alpha = jnp.exp2(m_old - m_new)
        p = jnp.exp2(s - m_new)
        l_df[...] = l_df[...] * alpha + jnp.sum(p, axis=1, keepdims=True)
        pv = [_dot(p[h * 2 * dec:(h + 1) * 2 * dec].astype(BF16), fetch(DF_HEADS + h))
              for h in range(DF_HEADS)]
        acc_df[...] = acc_df[...] * alpha + jnp.concatenate(pv, axis=0)
        m_df[...] = m_new

    @pl.when(j == 0)
    def _():
        acc_sb[...] = jnp.zeros_like(acc_sb)
        c_sb[...] = jnp.zeros_like(c_sb)
        acc_df[...] = jnp.zeros_like(acc_df)
        m_df[...] = jnp.full_like(m_df, NEG_BIG)
        l_df[...] = jnp.zeros_like(l_df)
        sb_new(nsb_ref)
        df_block([ndf_ref], dec, True)

    sb_pages_step(sb_pages)
    df_block(df_pages, page, False)

    @pl.when(j == pl.num_programs(1) - 1)
    def _():
        a = acc_sb[...]
        osb_ref[...] = jnp.concatenate(
            [a[h * dec:(h + 1) * dec, h * HEAD_DIM:(h + 1) * HEAD_DIM] for h in range(SB_HEADS)],
            axis=1).astype(osb_ref.dtype)
        d = acc_df[...] / l_df[...]
        lam = _lambda(lam_ref, lam_init)
        outs = []
        for h in range(DF_HEADS):
            o = d[h * 2 * dec:h * 2 * dec + dec] - lam * d[h * 2 * dec + dec:(h + 1) * 2 * dec]
            outs.append(_rms(o, g_ref[...]) * (1.0 - lam_init))
        odf_ref[...] = jnp.concatenate(outs, axis=1).astype(odf_ref.dtype)


def _sample_attention(page_table, lam_p, g_subln, q_sb, q_df, new_sb, new_df, cache_sb, cache_df,
                      lam_init, pages_per_step):
    n_seq, n_pages = page_table.shape
    dec = q_sb.shape[0] // n_seq
    n_pool, _, page = cache_sb.shape[:3]
    sb_rows = 2 * SB_WIDTH
    df_rows = page * 2 * DF_HEADS
    csb = jnp.transpose(cache_sb, (0, 1, 3, 4, 5, 2)).reshape(n_pool, sb_rows, page)
    cdf = cache_df.reshape(n_pool, df_rows, 2 * HEAD_DIM)
    ndf = new_df.reshape(n_seq, dec * 2 * DF_HEADS, 2 * HEAD_DIM)
    jj = lax.broadcasted_iota(jnp.int32, (page, 2 * page), 0)
    ss = lax.broadcasted_iota(jnp.int32, (page, 2 * page), 1)
    later = ((jj > ss) | (ss >= page)).astype(BF16)
    n_steps = n_pages // pages_per_step
    const = lambda s, j, pt: (0, 0)
    seq_blk = lambda s, j, pt: (s, 0)

    def page_map(r):
        return lambda s, j, pt: (pt[s, n_pages - 1 - (j * pages_per_step + r)], 0, 0)

    in_specs = [pl.BlockSpec(lam_p.shape, const), pl.BlockSpec(g_subln.shape, const),
                pl.BlockSpec(later.shape, const),
                pl.BlockSpec((dec, SB_WIDTH), seq_blk), pl.BlockSpec((dec, DF_WIDTH), seq_blk),
                pl.BlockSpec((dec, 2 * SB_WIDTH), seq_blk),
                pl.BlockSpec((None,) + ndf.shape[1:], lambda s, j, pt: (s, 0, 0))]
    in_specs += [pl.BlockSpec((None, sb_rows, page), page_map(r)) for r in range(pages_per_step)]
    in_specs += [pl.BlockSpec((None, df_rows, 2 * HEAD_DIM), page_map(r)) for r in range(pages_per_step)]
    n_sb, n_df = SB_HEADS * dec, DF_HEADS * 2 * dec
    grid_spec = pltpu.PrefetchScalarGridSpec(
        num_scalar_prefetch=1,
        grid=(n_seq, n_steps),
        in_specs=in_specs,
        out_specs=(pl.BlockSpec((dec, SB_WIDTH), seq_blk), pl.BlockSpec((dec, DF_WIDTH), seq_blk)),
        scratch_shapes=[pltpu.VMEM((n_sb, SB_WIDTH), F32), pltpu.VMEM((n_sb, LANES), F32),
                        pltpu.VMEM((n_df, LANES), F32), pltpu.VMEM((n_df, 1), F32),
                        pltpu.VMEM((n_df, 1), F32)])
    return pl.pallas_call(
        functools.partial(_sample_kernel, pages_per_step=pages_per_step, page=page, dec=dec,
                          lam_init=lam_init),
        grid_spec=grid_spec,
        out_shape=(jax.ShapeDtypeStruct((n_seq * dec, SB_WIDTH), F32),
                   jax.ShapeDtypeStruct((n_seq * dec, DF_WIDTH), F32)),
        compiler_params=_cparams(("parallel", "arbitrary")),
        name="sample_attn",
    )(page_table, lam_p, g_subln, later, q_sb, q_df, new_sb, ndf,
      *([csb] * pages_per_step), *([cdf] * pages_per_step))


def _route(logits):
    lane = lax.broadcasted_iota(jnp.int32, logits.shape, 1).astype(F32)
    big = 1e9

    def first_argmax(vals, valid):
        v = jnp.where(valid, vals, -jnp.inf)
        top = jnp.max(v, axis=1, keepdims=True)
        idx = jnp.min(jnp.where(valid & (v == top), lane, big), axis=1, keepdims=True)
        return top, idx

    is_grp = lane < N_GROUPS
    g_top, g_idx = first_argmax(logits, is_grp)
    p_g = 1.0 / jnp.sum(jnp.where(is_grp, jnp.exp(logits - g_top), 0.0), axis=1, keepdims=True)
    lo = N_GROUPS + g_idx * EXPERTS_PER_GROUP
    in_grp = (lane >= lo) & (lane < lo + EXPERTS_PER_GROUP)
    v1, i1 = first_argmax(logits, in_grp)
    v2, i2 = first_argmax(logits, in_grp & (lane != i1))
    e2 = jnp.exp(v2 - v1)
    w1 = p_g / (1.0 + e2)
    w2 = p_g * e2 / (1.0 + e2)
    comb = jnp.where(lane == i1, w1, 0.0) + jnp.where(lane == i2, w2, 0.0)
    return pltpu.roll(comb, LANES - N_GROUPS, 1)


def _merge_kernel(x_ref, osb_ref, odf_ref, gate_ref, wsb_ref, wdf_ref, wo_ref, g_ref, wr_ref,
                  xm_ref, t_ref, comb_ref):
    a = _dot(osb_ref[...].astype(BF16), wsb_ref[...])
    b = _dot(odf_ref[...].astype(BF16), wdf_ref[...])
    merged = gate_ref[:, :D_MODEL] * a + gate_ref[:, D_MODEL:] * b
    xm = x_ref[...] + _dot(merged.astype(BF16), wo_ref[...])
    xm_ref[...] = xm
    t = _rms(xm, g_ref[...])
    t_hi = t.astype(BF16)
    t_ref[...] = t_hi
    t_lo = (t - t_hi.astype(F32)).astype(BF16)
    logits = _dot(jnp.concatenate([t_hi, t_hi, t_lo], axis=1), wr_ref[...])
    comb_ref[...] = _route(logits)


def _merge(x, o_sb, o_df, gates, w_sb16, w_df16, w_o16, g_ffn, w_route, tm):
    n = x.shape[0]
    row = lambda i: (i, 0)
    const = lambda i: (0, 0)
    return pl.pallas_call(
        _merge_kernel,
        grid=(n // tm,),
        in_specs=[pl.BlockSpec((tm, D_MODEL), row), pl.BlockSpec((tm, SB_WIDTH), row),
                  pl.BlockSpec((tm, DF_WIDTH), row), pl.BlockSpec((tm, 2 * D_MODEL), row),
                  pl.BlockSpec(w_sb16.shape, const), pl.BlockSpec(w_df16.shape, const),
                  pl.BlockSpec(w_o16.shape, const), pl.BlockSpec((1, D_MODEL), const),
                  pl.BlockSpec(w_route.shape, const)],
        out_specs=(pl.BlockSpec((tm, D_MODEL), row), pl.BlockSpec((tm, D_MODEL), row),
                   pl.BlockSpec((tm, LANES), row)),
        out_shape=(jax.ShapeDtypeStruct((n, D_MODEL), F32), jax.ShapeDtypeStruct((n, D_MODEL), BF16),
                   jax.ShapeDtypeStruct((n, LANES), F32)),
        compiler_params=_cparams(("parallel",)),
        name="merge_route",
    )(x, o_sb, o_df, gates, w_sb16, w_df16, w_o16, g_ffn, w_route)


def _moe_kernel(t_ref, comb_ref, xm_ref, wg_ref, wu_ref, wd_ref, o_ref):
    g = pl.program_id(1)

    @pl.when(g == 0)
    def _():
        o_ref[...] = xm_ref[...]

    t = t_ref[...]
    comb = comb_ref[...]
    lane = lax.broadcasted_iota(jnp.int32, comb.shape, 1)
    acc = jnp.zeros(o_ref.shape, F32)
    for e in range(EXPERTS_PER_GROUP):
        gate = _dot(t, wg_ref[e])
        up = _dot(t, wu_ref[e])
        hid = gate * jax.nn.sigmoid(gate) * up
        c = jnp.sum(jnp.where(lane == g * EXPERTS_PER_GROUP + e, comb, 0.0), axis=1, keepdims=True)
        acc = acc + c * _dot(hid.astype(BF16), wd_ref[e])
    o_ref[...] += acc


def _moe(t, comb, xm, w_gate16, w_up16, w_down16, tm):
    n = t.shape[0]
    row = lambda i, g: (i, 0)
    grp = lambda i, g: (g, 0, 0)
    return pl.pallas_call(
        _moe_kernel,
        grid=(n // tm, N_GROUPS),
        in_specs=[pl.BlockSpec((tm, D_MODEL), row), pl.BlockSpec((tm, LANES), row),
                  pl.BlockSpec((tm, D_MODEL), row),
                  pl.BlockSpec((EXPERTS_PER_GROUP, D_MODEL, D_EXPERT), grp),
                  pl.BlockSpec((EXPERTS_PER_GROUP, D_MODEL, D_EXPERT), grp),
                  pl.BlockSpec((EXPERTS_PER_GROUP, D_EXPERT, D_MODEL), grp)],
        out_specs=pl.BlockSpec((tm, D_MODEL), row),
        out_shape=jax.ShapeDtypeStruct((n, D_MODEL), F32),
        compiler_params=_cparams(("parallel", "arbitrary")),
        name="moe",
    )(t, comb, xm, w_gate16, w_up16, w_down16)


def _ple_kernel(x_ref, p_ref, gp_ref, wg_ref, wp_ref, gf_ref, y_ref):
    x = x_ref[...]
    gate = jax.nn.sigmoid(_dot(_rms(x, gp_ref[...]).astype(BF16), wg_ref[...]))
    x = x + gate * _dot(p_ref[...].astype(BF16), wp_ref[...])
    y_ref[...] = _rms(x, gf_ref[...])


def _ple(x, p, g_ple, w_gate16, w_proj16, g_final, tm):
    n = x.shape[0]
    row = lambda i: (i, 0)
    const = lambda i: (0, 0)
    return pl.pallas_call(
        _ple_kernel,
        grid=(n // tm,),
        in_specs=[pl.BlockSpec((tm, D_MODEL), row), pl.BlockSpec((tm, p.shape[1]), row),
                  pl.BlockSpec((1, D_MODEL), const), pl.BlockSpec(w_gate16.shape, const),
                  pl.BlockSpec(w_proj16.shape, const), pl.BlockSpec((1, D_MODEL), const)],
        out_specs=pl.BlockSpec((tm, D_MODEL), row),
        out_shape=jax.ShapeDtypeStruct((n, D_MODEL), F32),
        compiler_params=_cparams(("parallel",)),
        name="ple_final",
    )(x, p, g_ple, w_gate16, w_proj16, g_final)


def _pick_tile(n, prefer):
    for t in prefer:
        if n % t == 0:
            return t
    raise ValueError(f"no row tile for {n} rows")


def kernel(x_prompt, x_sample, p_prompt, p_sample, cache_sb, cache_df, page_table, g_mix, w_in, lam_q1, lam_k1, lam_q2, lam_k2, g_subln, w_sb, w_df, w_o, g_ffn, w_grp, w_rtr, w_gate, w_up, w_down, g_ple, w_ple_gate, w_ple_proj, g_final):
    depth = g_mix.shape[0]
    assert depth == 1, "single-layer step only"
    batch, seq, _ = x_prompt.shape
    n_seq, dec, _ = x_sample.shape
    n_pages = page_table.shape[1]
    page = cache_sb.shape[2]
    past = n_pages * page
    lam_init = 0.8 - 0.6 * math.exp(-0.3 * 0)

    w_in16 = w_in[0].astype(BF16)
    w_sb16, w_df16, w_o16 = w_sb[0].astype(BF16), w_df[0].astype(BF16), w_o[0].astype(BF16)
    w_gate16, w_up16, w_down16 = w_gate[0].astype(BF16), w_up[0].astype(BF16), w_down[0].astype(BF16)
    w_pg16, w_pp16 = w_ple_gate[0].astype(BF16), w_ple_proj[0].astype(BF16)
    w_route = jnp.concatenate(
        [w_grp[0], jnp.transpose(w_rtr[0], (1, 0, 2)).reshape(D_MODEL, N_EXPERTS),
         jnp.zeros((D_MODEL, LANES - N_GROUPS - N_EXPERTS), F32)], axis=1)
    w_route_hi = w_route.astype(BF16)
    w_route_lo = (w_route - w_route_hi.astype(F32)).astype(BF16)
    w_route = jnp.concatenate([w_route_hi, w_route_lo, w_route_hi], axis=0)
    lam_p =jnp.concatenate([lam_q1, lam_k1, lam_q2, lam_k2], axis=0).astype(F32)
    g_final2 = g_final.reshape(1, D_MODEL)

    def finish(x, o_sb, o_df, gates, p):
        tm = _pick_tile(x.shape[0], (512, 256, 128, 64, 32, 16, 8))
        xm, t, comb = _merge(x, o_sb, o_df, gates, w_sb16, w_df16, w_o16, g_ffn, w_route, tm)
        x2 = _moe(t, comb, xm, w_gate16, w_up16, w_down16, tm)
        return _ple(x2, p, g_ple, w_pg16, w_pp16, g_final2, tm)

    n_p = batch * seq
    tm_p = _pick_tile(seq, (256, 128))
    tq = _pick_tile(seq, (256, 128))
    xp = x_prompt.reshape(n_p, D_MODEL)
    tabs_p = _rope_tables(jnp.arange(seq, dtype=jnp.int32))
    q_sb, q_df, new_sb_p, new_df_p, sb16, df16, gates = _project(
        xp, g_mix, w_in16, tabs_p, seq // tm_p, BF16, tm_p, tokens_on_lanes_seq=seq)
    o_sb = _sb_prompt(q_sb, sb16, batch, seq, tq, 2)
    o_df = _df_prompt(lam_p, g_subln, q_df, df16, batch, seq, tq, 2, lam_init)
    y_prompt = finish(xp, o_sb, o_df, gates, p_prompt[0].reshape(n_p, -1)).reshape(x_prompt.shape)

    n_s = n_seq * dec
    tm_s = _pick_tile(n_s, (256, 128, 64, 32, 16, 8))
    xs = x_sample.reshape(n_s, D_MODEL)
    pos_s = past + jnp.arange(dec, dtype=jnp.int32)
    tabs_s = tuple(jnp.tile(t, (tm_s // dec, 1)) for t in _rope_tables(pos_s))
    q_sb, q_df, new_sb_s, new_df_s, _, _, gates = _project(xs, g_mix, w_in16, tabs_s, 1, F32, tm_s)
    pages_per_step = next(c for c in (8, 4, 2, 1) if n_pages % c == 0)
    o_sb, o_df = _sample_attention(page_table, lam_p, g_subln, q_sb, q_df, new_sb_s, new_df_s,
                                   cache_sb, cache_df, lam_init, pages_per_step)
    y_sample = finish(xs, o_sb, o_df, gates, p_sample[0].reshape(n_s, -1)).reshape(x_sample.shape)

    return (y_prompt, y_sample,
            jnp.transpose(new_sb_p.reshape(batch, 1, 2, SB_HEADS, HEAD_DIM, seq), (0, 1, 5, 2, 3, 4)),
            new_df_p.reshape(batch, 1, seq, 2, DF_HEADS, 2 * HEAD_DIM),
            new_sb_s.reshape(n_seq, 1, dec, 2, SB_HEADS, HEAD_DIM),
            new_df_s.reshape(n_seq, 1, dec, 2, DF_HEADS, 2 * HEAD_DIM))
```

```python
import functools
import math

import jax
import jax.numpy as jnp
from jax import lax
from jax.experimental import pallas as pl
from jax.experimental.pallas import tpu as pltpu

F32 = jnp.float32
BF16 = jnp.bfloat16

RMS_EPS = 1e-6
ROPE_THETA = 10000.0
D_MODEL = 1024
SB_HEADS = 8
DF_HEADS = 4
HEAD_DIM = 64
SB_WIDTH = SB_HEADS * HEAD_DIM
DF_WIDTH = DF_HEADS * 2 * HEAD_DIM
N_GROUPS = 4
EXPERTS_PER_GROUP = 8
N_EXPERTS = N_GROUPS * EXPERTS_PER_GROUP
D_EXPERT = 256
LANES = 128
NEG_BIG = -1e30
SCORE_SCALE = HEAD_DIM ** -0.5 * math.log2(math.e)
GROUP_LANE = N_EXPERTS
SUBLANE_PACK = 16
VMEM_LIMIT = 56 * 1024 * 1024


def _cparams(sem):
    return pltpu.CompilerParams(dimension_semantics=sem, vmem_limit_bytes=VMEM_LIMIT)


def _rms(x, g):
    ms = jnp.mean(x * x, axis=-1, keepdims=True)
    return x * lax.rsqrt(ms + RMS_EPS) * g


def _dot(a, b):
    return jnp.dot(a, b, preferred_element_type=F32)


def _dot_nt(a, b):
    return lax.dot_general(a, b, (((1,), (1,)), ((), ())), preferred_element_type=F32)


def _proj_kernel(x_ref, g_ref, w_ref, cos_ref, sa_ref, sb_ref,
                 qsb_ref, qdf_ref, nsb_ref, ndf_ref, sb16_ref, df16_ref, gate_ref, *, tokens_on_lanes):
    tm = x_ref.shape[0]
    h = _rms(x_ref[...], g_ref[...]).astype(BF16)

    def put_sb(part, t):
        if tokens_on_lanes:
            nsb_ref[part * SB_WIDTH:(part + 1) * SB_WIDTH, :] = t.T
        else:
            nsb_ref[:, part * SB_WIDTH:(part + 1) * SB_WIDTH] = t
        sb16_ref[:, part * SB_WIDTH:(part + 1) * SB_WIDTH] = t.astype(BF16)

    def put_df(part, t):
        for hd in range(DF_HEADS):
            ndf_ref[pl.ds(part * DF_HEADS + hd, tm, stride=2 * DF_HEADS), :] = t[:, hd * LANES:(hd + 1) * LANES]
        df16_ref[:, part * DF_WIDTH:(part + 1) * DF_WIDTH] = t.astype(BF16)

    cos, sa, sb = cos_ref[...], sa_ref[...], sb_ref[...]

    def mm(c):
        return _dot(h, w_ref[:, c * 1024:(c + 1) * 1024])

    def rope(t):
        parts = []
        for c in range(t.shape[1] // LANES):
            u = t[:, c * LANES:(c + 1) * LANES]
            parts.append(u * cos + pltpu.roll(u, 96, 1) * sa + pltpu.roll(u, 32, 1) * sb)
        return jnp.concatenate(parts, axis=1)

    z = mm(0)
    qsb_ref[...] = (z[:, :512] * SCORE_SCALE).astype(qsb_ref.dtype)
    put_sb(0, z[:, 512:])
    z = mm(1)
    put_sb(1, z[:, :512])
    qdf_ref[...] = (rope(z[:, 512:]) * SCORE_SCALE).astype(qdf_ref.dtype)
    z = mm(2)
    put_df(0, rope(z[:, :512]))
    put_df(1, z[:, 512:])
    gate_ref[:, :1024] = jax.nn.sigmoid(mm(3))
    gate_ref[:, 1024:] = jax.nn.sigmoid(mm(4))


def _project(x, g, w_in16, tables, period_blocks, q_dtype, tm, tokens_on_lanes_seq=None):
    n = x.shape[0]
    cos, sa, sb = tables
    row = lambda i: (i, 0)
    const = lambda i: (0, 0)
    tab = lambda i: (i % period_blocks, 0)
    if tokens_on_lanes_seq is None:
        nsb_shape, nsb_spec = (n, 2 * SB_WIDTH), pl.BlockSpec((tm, 2 * SB_WIDTH), row)
    else:
        per_seq = tokens_on_lanes_seq // tm
        nsb_shape = (n // tokens_on_lanes_seq, 2 * SB_WIDTH, tokens_on_lanes_seq)
        nsb_spec = pl.BlockSpec((None, 2 * SB_WIDTH, tm), lambda i: (i // per_seq, 0, i % per_seq))
    rows_df = 2 * DF_HEADS
    out_shape = (
        jax.ShapeDtypeStruct((n, 512), q_dtype), jax.ShapeDtypeStruct((n, 512), q_dtype),
        jax.ShapeDtypeStruct(nsb_shape, F32), jax.ShapeDtypeStruct((n * rows_df, 2 * HEAD_DIM), F32),
        jax.ShapeDtypeStruct((n, 1024), BF16), jax.ShapeDtypeStruct((n, 1024), BF16),
        jax.ShapeDtypeStruct((n, 2048), F32))
    return pl.pallas_call(
        functools.partial(_proj_kernel, tokens_on_lanes=tokens_on_lanes_seq is not None),
        grid=(n // tm,),
        in_specs=[pl.BlockSpec((tm, D_MODEL), row), pl.BlockSpec((1, D_MODEL), const),
                  pl.BlockSpec(w_in16.shape, const, pipeline_mode=pl.Buffered(1)),
                  pl.BlockSpec((tm, LANES), tab), pl.BlockSpec((tm, LANES), tab),
                  pl.BlockSpec((tm, LANES), tab)],
        out_specs=(pl.BlockSpec((tm, 512), row), pl.BlockSpec((tm, 512), row),
                   nsb_spec, pl.BlockSpec((tm * rows_df, 2 * HEAD_DIM), row),
                   pl.BlockSpec((tm, 1024), row), pl.BlockSpec((tm, 1024), row),
                   pl.BlockSpec((tm, 2048), row)),
        out_shape=out_shape,
        compiler_params=_cparams(("parallel",)),
        name="proj",
    )(x, g, w_in16, cos, sa, sb)


def _rope_tables(pos):
    half = HEAD_DIM // 2
    inv = 1.0 / (ROPE_THETA ** (jnp.arange(half, dtype=F32) * 2.0 / HEAD_DIM))
    ang = pos.astype(F32)[:, None] * inv[None, :]
    cos, sin = jnp.cos(ang), jnp.sin(ang)
    zero = jnp.zeros_like(sin)
    cos64 = jnp.concatenate([cos, cos], axis=1)
    sa64 = jnp.concatenate([-sin, zero], axis=1)
    sb64 = jnp.concatenate([zero, sin], axis=1)
    rep = lambda t: jnp.concatenate([t, t], axis=1)
    return rep(cos64), rep(sa64), rep(sb64)


def _sb_prompt_kernel(q_ref, k_ref, v_ref, o_ref, *, tq, cols):
    i = pl.program_id(2)
    lane = lax.broadcasted_iota(jnp.int32, (tq, LANES), 1)
    row = lax.broadcasted_iota(jnp.int32, (tq, tq), 0)
    col = lax.broadcasted_iota(jnp.int32, (tq, tq), 1)
    later = (row > col).astype(BF16)
    causal = col < row
    heads = []
    for cb in range(cols):
        q = q_ref[:, cb * LANES:(cb + 1) * LANES].astype(F32)
        heads.append((cb, jnp.where(lane < HEAD_DIM, q, 0.0).astype(BF16)))
        heads.append((cb, jnp.where(lane >= HEAD_DIM, q, 0.0).astype(BF16)))

    def tile(kb, state, masked):
        start = pl.multiple_of(kb * tq, tq)
        new = []
        for (cb, qh), (c, acc) in zip(heads, state):
            k = k_ref[pl.ds(start, tq), cb * LANES:(cb + 1) * LANES]
            v = v_ref[pl.ds(start, tq), cb * LANES:(cb + 1) * LANES]
            z = _dot_nt(qh, k)
            soft = jnp.log2(1.0 + jnp.exp2(-jnp.abs(z)))
            log_beta = jnp.minimum(z, 0.0) - soft
            log_1mb = log_beta - z
            if masked:
                log_1mb = jnp.where(causal, log_1mb, 0.0)
            rem = _dot(log_1mb.astype(BF16), later)
            w = jnp.exp2(log_beta + rem + c)
            if masked:
                w = jnp.where(causal, w, 0.0)
            acc = acc + _dot(w.astype(BF16), v)
            c = c + jnp.sum(log_1mb, axis=1, keepdims=True)
            new.append((c, acc))
        return tuple(new)

    init = tuple((jnp.zeros((tq, 1), F32), jnp.zeros((tq, LANES), F32)) for _ in heads)
    state = tile(i, init, True)
    state = lax.fori_loop(0, i, lambda s, st: tile(i - 1 - s, st, False), state)
    for cb in range(cols):
        o_ref[:, cb * LANES:(cb + 1) * LANES] = jnp.where(
            lane < HEAD_DIM, state[2 * cb][1], state[2 * cb + 1][1]).astype(o_ref.dtype)


def _sb_prompt(q, kv16, batch, seq, tq, cols):
    nq = seq // tq
    width = cols * LANES
    n_blk = SB_WIDTH // width
    return pl.pallas_call(
        functools.partial(_sb_prompt_kernel, tq=tq, cols=cols),
        grid=(batch, n_blk, nq),
        in_specs=[pl.BlockSpec((tq, width), lambda b, p, i: (b * nq + i, p)),
                  pl.BlockSpec((seq, width), lambda b, p, i: (b, p)),
                  pl.BlockSpec((seq, width), lambda b, p, i: (b, n_blk + p))],
        out_specs=pl.BlockSpec((tq, width), lambda b, p, i: (b * nq + i, p)),
        out_shape=jax.ShapeDtypeStruct((batch * seq, SB_WIDTH), BF16),
        compiler_params=_cparams(("parallel", "parallel", "arbitrary")),
        name="sb_prompt",
    )(q, kv16, kv16)


def _lambda(lam_ref, lam_init):
    lp = lam_ref[...]
    a = jnp.sum(lp[0:1] * lp[1:2], axis=1, keepdims=True)
    b = jnp.sum(lp[2:3] * lp[3:4], axis=1, keepdims=True)
    return jnp.exp(a) - jnp.exp(b) + lam_init


def _df_prompt_kernel(lam_ref, g_ref, q_ref, k_ref, v_ref, o_ref, *, tq, cols, lam_init):
    i = pl.program_id(2)
    lane = lax.broadcasted_iota(jnp.int32, (tq, LANES), 1)
    row = lax.broadcasted_iota(jnp.int32, (tq, tq), 0)
    col = lax.broadcasted_iota(jnp.int32, (tq, tq), 1)
    causal = col <= row
    halves = []
    for cb in range(cols):
        q = q_ref[:, cb * LANES:(cb + 1) * LANES].astype(F32)
        halves.append((cb, jnp.where(lane < HEAD_DIM, q, 0.0).astype(BF16)))
        halves.append((cb, jnp.where(lane >= HEAD_DIM, q, 0.0).astype(BF16)))

    ones = jnp.ones((tq, LANES), BF16)

    def tile(kb, st, masked):
        start = pl.multiple_of(kb * tq, tq)
        new = []
        for (cb, qx), (m, acc) in zip(halves, st):
            k = k_ref[pl.ds(start, tq), cb * LANES:(cb + 1) * LANES]
            v = v_ref[pl.ds(start, tq), cb * LANES:(cb + 1) * LANES]
            s = _dot_nt(qx, k)
            if masked:
                s = jnp.where(causal, s, NEG_BIG)
            m_new = jnp.maximum(m, jnp.max(s, axis=1, keepdims=True))
            alpha = jnp.exp2(m - m_new)
            p = jnp.exp2(s - m_new)
            acc = acc * alpha + _dot(p.astype(BF16), jnp.concatenate([v, ones], axis=1))
            new.append((m_new, acc))
        return tuple(new)

    init = (jnp.full((tq, 1), NEG_BIG, F32), jnp.zeros((tq, 2 * LANES), F32))
    st = tile(i, (init,) * len(halves), True)
    st = lax.fori_loop(0, i, lambda s, c: tile(i - 1 - s, c, False), st)
    lam = _lambda(lam_ref, lam_init)
    for cb in range(cols):
        (_, e1), (_, e2) = st[2 * cb], st[2 * cb + 1]
        o = e1[:, :LANES] / e1[:, LANES:] - lam * (e2[:, :LANES] / e2[:, LANES:])
        o_ref[:, cb * LANES:(cb + 1) * LANES] = (_rms(o, g_ref[...]) * (1.0 - lam_init)).astype(o_ref.dtype)


def _df_prompt(lam_p, g_subln, q, kv16, batch, seq, tq, cols, lam_init):
    nq = seq // tq
    width = cols * LANES
    n_blk = DF_WIDTH // width
    const = lambda b, h, i: (0, 0)
    return pl.pallas_call(
        functools.partial(_df_prompt_kernel, tq=tq, cols=cols, lam_init=lam_init),
        grid=(batch, n_blk, nq),
        in_specs=[pl.BlockSpec(lam_p.shape, const), pl.BlockSpec(g_subln.shape, const),
                  pl.BlockSpec((tq, width), lambda b, h, i: (b * nq + i, h)),
                  pl.BlockSpec((seq, width), lambda b, h, i: (b, h)),
                  pl.BlockSpec((seq, width), lambda b, h, i: (b, n_blk + h))],
        out_specs=pl.BlockSpec((tq, width), lambda b, h, i: (b * nq + i, h)),
        out_shape=jax.ShapeDtypeStruct((batch * seq, DF_WIDTH), BF16),
        compiler_params=_cparams(("parallel", "parallel", "arbitrary")),
        name="df_prompt",
    )(lam_p, g_subln, q, kv16, kv16)


def _sample_kernel(pt_ref, lam_ref, g_ref, later_ref, qsb_ref, qdf_ref, nsb_ref, ndf_ref, *rest,
                   pages_per_step, page, dec, lam_init):
    sb_pages = rest[:pages_per_step]
    df_pages = rest[pages_per_step:2 * pages_per_step]
    osb_ref, odf_ref = rest[2 * pages_per_step:2 * pages_per_step + 2]
    acc_sb, c_sb, acc_df, m_df, l_df = rest[2 * pages_per_step + 2:]
    j = pl.program_id(1)
    n_sb = SB_HEADS * dec
    n_df = DF_HEADS * 2 * dec
    lane = lax.broadcasted_iota(jnp.int32, (dec, LANES), 1)

    qrow = lax.broadcasted_iota(jnp.int32, (n_sb, SB_WIDTH), 0) // dec
    qcol = lax.broadcasted_iota(jnp.int32, (n_sb, SB_WIDTH), 1) // HEAD_DIM
    q_sb = jnp.where(qrow == qcol, jnp.concatenate([qsb_ref[...]] * SB_HEADS, axis=0), 0.0).astype(BF16)
    q_df = qdf_ref[...]
    q_df_heads = []
    for h in range(DF_HEADS):
        qh = q_df[:, h * LANES:(h + 1) * LANES]
        q_df_heads.append(jnp.concatenate(
            [jnp.where(lane < HEAD_DIM, qh, 0.0), jnp.where(lane >= HEAD_DIM, qh, 0.0)], axis=0).astype(BF16))

    def pad_tokens(t):
        if t.shape[0] == page:
            return t
        return jnp.concatenate([t, jnp.zeros((page - t.shape[0], t.shape[1]), t.dtype)], axis=0)

    def sb_weights(z, masked):
        n = z.shape[1] // page
        soft = jnp.log2(1.0 + jnp.exp2(-jnp.abs(z)))
        log_beta = jnp.minimum(z, 0.0) - soft
        log_1mb = log_beta - z
        if masked:
            rows = lax.broadcasted_iota(jnp.int32, z.shape, 0) % dec
            cols = lax.broadcasted_iota(jnp.int32, z.shape, 1)
            valid = cols < rows
            log_1mb = jnp.where(valid, log_1mb, 0.0)
        hi = log_1mb.astype(BF16).astype(F32)
        parts = [t[:, r * page:(r + 1) * page] for t in (hi, log_1mb - hi) for r in range(n)]
        both = _dot(jnp.concatenate(parts, axis=0).astype(BF16), later_ref[...])
        c = c_sb[...]
        ws = []
        for r in range(n):
            top = both[r * n_sb:(r + 1) * n_sb] + both[(n + r) * n_sb:(n + r + 1) * n_sb]
            ws.append(jnp.exp2(log_beta[:, r * page:(r + 1) * page] + top[:, :page] + c))
            c = c + top[:, page:]
        c_sb[...] = c
        w = ws[0] if n == 1 else jnp.concatenate(ws, axis=1)
        if masked:
            w = jnp.where(valid, w, 0.0)
        return w.astype(BF16)

    def sb_pages_step(refs):
        kt = jnp.concatenate([ref[:SB_WIDTH, :].astype(BF16) for ref in refs], axis=1)
        w = sb_weights(_dot(q_sb, kt), False)
        vt = jnp.concatenate([ref[SB_WIDTH:, :].astype(BF16) for ref in refs], axis=1)
        acc_sb[...] += _dot_nt(w, vt)

    def sb_new(ref):
        w = sb_weights(_dot_nt(q_sb, pad_tokens(ref[:, :SB_WIDTH]).astype(BF16)), True)
        acc_sb[...] += _dot(w, pad_tokens(ref[:, SB_WIDTH:]).astype(BF16))

    def df_block(refs, ntok, masked):
        def fetch(r):
            rows = [pad_tokens(ref[pl.ds(r, ntok, stride=2 * DF_HEADS), :]).astype(BF16) for ref in refs]
            return rows[0] if len(rows) == 1 else jnp.concatenate(rows, axis=0)
        s = jnp.concatenate([_dot_nt(q_df_heads[h], fetch(h)) for h in range(DF_HEADS)], axis=0)
        if masked:
            rows = lax.broadcasted_iota(jnp.int32, s.shape, 0) % dec
            cols = lax.broadcasted_iota(jnp.int32, s.shape, 1)
            s = jnp.where(cols <= rows, s, NEG_BIG)
        m_old = m_df[...]
        m_new = jnp.maximum(m_old, jnp.max(s, axis=1, keepdims=True))
        alpha = jnp.exp2(m_old - m_new)
        p = jnp.exp2(s - m_new)
        l_df[...] = l_df[...] * alpha + jnp.sum(p, axis=1, keepdims=True)
        pv = [_dot(p[h * 2 * dec:(h + 1) * 2 * dec].astype(BF16), fetch(DF_HEADS + h))
              for h in range(DF_HEADS)]
        acc_df[...] = acc_df[...] * alpha + jnp.concatenate(pv, axis=0)
        m_df[...] = m_new

    @pl.when(j == 0)
    def _():
        acc_sb[...] = jnp.zeros_like(acc_sb)
        c_sb[...] = jnp.zeros_like(c_sb)
        acc_df[...] = jnp.zeros_like(acc_df)
        m_df[...] = jnp.full_like(m_df, NEG_BIG)
        l_df[...] = jnp.zeros_like(l_df)
        sb_new(nsb_ref)
        df_block([ndf_ref], dec, True)

    sb_pages_step(sb_pages)
    df_block(df_pages, page, False)

    @pl.when(j == pl.num_programs(1) - 1)
    def _():
        a = acc_sb[...]
        osb_ref[...] = jnp.concatenate(
            [a[h * dec:(h + 1) * dec, h * HEAD_DIM:(h + 1) * HEAD_DIM] for h in range(SB_HEADS)],
            axis=1).astype(osb_ref.dtype)
        d = acc_df[...] / l_df[...]
        lam = _lambda(lam_ref, lam_init)
        outs = []
        for h in range(DF_HEADS):
            o = d[h * 2 * dec:h * 2 * dec + dec] - lam * d[h * 2 * dec + dec:(h + 1) * 2 * dec]
            outs.append(_rms(o, g_ref[...]) * (1.0 - lam_init))
        odf_ref[...] = jnp.concatenate(outs, axis=1).astype(odf_ref.dtype)


def _sample_attention(page_table, lam_p, g_subln, q_sb, q_df, new_sb, new_df, cache_sb, cache_df,
                      lam_init, pages_per_step):
    n_seq, n_pages = page_table.shape
    dec = q_sb.shape[0] // n_seq
    n_pool, _, page = cache_sb.shape[:3]
    sb_rows = 2 * SB_WIDTH
    df_rows = page * 2 * DF_HEADS
    csb = jnp.transpose(cache_sb, (0, 1, 3, 4, 5, 2)).reshape(n_pool, sb_rows, page)
    cdf = cache_df.reshape(n_pool, df_rows, 2 * HEAD_DIM)
    ndf = new_df.reshape(n_seq, dec * 2 * DF_HEADS, 2 * HEAD_DIM)
    jj = lax.broadcasted_iota(jnp.int32, (page, 2 * page), 0)
    ss = lax.broadcasted_iota(jnp.int32, (page, 2 * page), 1)
    later = ((jj > ss) | (ss >= page)).astype(BF16)
    n_steps = n_pages // pages_per_step
    const = lambda s, j, pt: (0, 0)
    seq_blk = lambda s, j, pt: (s, 0)

    def page_map(r):
        return lambda s, j, pt: (pt[s, n_pages - 1 - (j * pages_per_step + r)], 0, 0)

    in_specs = [pl.BlockSpec(lam_p.shape, const), pl.BlockSpec(g_subln.shape, const),
                pl.BlockSpec(later.shape, const),
                pl.BlockSpec((dec, SB_WIDTH), seq_blk), pl.BlockSpec((dec, DF_WIDTH), seq_blk),
                pl.BlockSpec((dec, 2 * SB_WIDTH), seq_blk),
                pl.BlockSpec((None,) + ndf.shape[1:], lambda s, j, pt: (s, 0, 0))]
    in_specs += [pl.BlockSpec((None, sb_rows, page), page_map(r)) for r in range(pages_per_step)]
    in_specs += [pl.BlockSpec((None, df_rows, 2 * HEAD_DIM), page_map(r)) for r in range(pages_per_step)]
    n_sb, n_df = SB_HEADS * dec, DF_HEADS * 2 * dec
    grid_spec = pltpu.PrefetchScalarGridSpec(
        num_scalar_prefetch=1,
        grid=(n_seq, n_steps),
        in_specs=in_specs,
        out_specs=(pl.BlockSpec((dec, SB_WIDTH), seq_blk), pl.BlockSpec((dec, DF_WIDTH), seq_blk)),
        scratch_shapes=[pltpu.VMEM((n_sb, SB_WIDTH), F32), pltpu.VMEM((n_sb, LANES), F32),
                        pltpu.VMEM((n_df, LANES), F32), pltpu.VMEM((n_df, 1), F32),
                        pltpu.VMEM((n_df, 1), F32)])
    return pl.pallas_call(
        functools.partial(_sample_kernel, pages_per_step=pages_per_step, page=page, dec=dec,
                          lam_init=lam_init),
        grid_spec=grid_spec,
        out_shape=(jax.ShapeDtypeStruct((n_seq * dec, SB_WIDTH), F32),
                   jax.ShapeDtypeStruct((n_seq * dec, DF_WIDTH), F32)),
        compiler_params=_cparams(("parallel", "arbitrary")),
        name="sample_attn",
    )(page_table, lam_p, g_subln, later, q_sb, q_df, new_sb, ndf,
      *([csb] * pages_per_step), *([cdf] * pages_per_step))


def _route(logits):
    lane = lax.broadcasted_iota(jnp.int32, logits.shape, 1).astype(F32)
    big = 1e9

    def first_argmax(vals, valid):
        v = jnp.where(valid, vals, -jnp.inf)
        top = jnp.max(v, axis=1, keepdims=True)
        idx = jnp.min(jnp.where(valid & (v == top), lane, big), axis=1, keepdims=True)
        return top, idx

    is_grp = lane < N_GROUPS
    g_top, g_idx = first_argmax(logits, is_grp)
    p_g = 1.0 / jnp.sum(jnp.where(is_grp, jnp.exp(logits - g_top), 0.0), axis=1, keepdims=True)
    lo = N_GROUPS + g_idx * EXPERTS_PER_GROUP
    in_grp = (lane >= lo) & (lane < lo + EXPERTS_PER_GROUP)
    v1, i1 = first_argmax(logits, in_grp)
    v2, i2 = first_argmax(logits, in_grp & (lane != i1))
    e2 = jnp.exp(v2 - v1)
    w1 = p_g / (1.0 + e2)
    w2 = p_g * e2 / (1.0 + e2)
    comb = jnp.where(lane == i1, w1, 0.0) + jnp.where(lane == i2, w2, 0.0)
    comb = pltpu.roll(comb, LANES - N_GROUPS, 1)
    comb = jnp.where(lane == GROUP_LANE, g_idx, comb)
    counts = jnp.sum(jnp.where(lane == g_idx, 1.0, 0.0), axis=0, keepdims=True)
    return comb, counts


def _merge_kernel(x_ref, osb_ref, odf_ref, gate_ref, wsb_ref, wdf_ref, wo_ref, g_ref, wr_ref,
                  xm_ref, t_ref, comb_ref, cnt_ref):
    a = _dot(osb_ref[...].astype(BF16), wsb_ref[...])
    b = _dot(odf_ref[...].astype(BF16), wdf_ref[...])
    merged = gate_ref[:, :D_MODEL] * a + gate_ref[:, D_MODEL:] * b
    xm = x_ref[...] + _dot(merged.astype(BF16), wo_ref[...])
    xm_ref[...] = xm
    t = _rms(xm, g_ref[...])
    t_hi = t.astype(BF16)
    t_ref[...] = t_hi
    t_lo = (t - t_hi.astype(F32)).astype(BF16)
    logits = _dot(jnp.concatenate([t_hi, t_hi, t_lo], axis=1), wr_ref[...])
    comb, counts = _route(logits)
    comb_ref[...] = comb
    cnt_ref[...] = jnp.broadcast_to(counts, cnt_ref.shape).astype(jnp.int32)


def _merge(x, o_sb, o_df, gates, w_sb16, w_df16, w_o16, g_ffn, w_route, tm):
    n = x.shape[0]
    row = lambda i: (i, 0)
    const = lambda i: (0, 0)
    return pl.pallas_call(
        _merge_kernel,
        grid=(n // tm,),
        in_specs=[pl.BlockSpec((tm, D_MODEL), row), pl.BlockSpec((tm, SB_WIDTH), row),
                  pl.BlockSpec((tm, DF_WIDTH), row), pl.BlockSpec((tm, 2 * D_MODEL), row),
                  pl.BlockSpec(w_sb16.shape, const), pl.BlockSpec(w_df16.shape, const),
                  pl.BlockSpec(w_o16.shape, const), pl.BlockSpec((1, D_MODEL), const),
                  pl.BlockSpec(w_route.shape, const)],
        out_specs=(pl.BlockSpec((tm, D_MODEL), row), pl.BlockSpec((tm, D_MODEL), row),
                   pl.BlockSpec((tm, LANES), row), pl.BlockSpec((None, 8, LANES), lambda i: (i, 0, 0))),
        out_shape=(jax.ShapeDtypeStruct((n, D_MODEL), F32), jax.ShapeDtypeStruct((n, D_MODEL), BF16),
                   jax.ShapeDtypeStruct((n, LANES), F32),
                   jax.ShapeDtypeStruct((n // tm, 8, LANES), jnp.int32)),
        compiler_params=_cparams(("parallel",)),
        name="merge_route",
    )(x, o_sb, o_df, gates, w_sb16, w_df16, w_o16, g_ffn, w_route)


def _moe_kernel(cnt_ref, t_ref, comb_ref, tri_ref, wg_ref, wu_ref, wd_ref, o_ref,
                xs_ref, cs_ref, ys_ref, p_ref, *, tm, win, sub_tiles):
    i = pl.program_id(0)
    g = pl.program_id(1)
    counts = [sum(cnt_ref[i * sub_tiles + s, gg] for s in range(sub_tiles)) for gg in range(N_GROUPS)]
    offsets = [sum(counts[:gg], jnp.int32(0)) for gg in range(N_GROUPS)]

    @pl.when(g == 0)
    def _():
        comb = comb_ref[...]
        gid = comb.T[GROUP_LANE:GROUP_LANE + 1, :]
        grp = lax.broadcasted_iota(jnp.int32, (8, tm), 0).astype(F32)
        before = _dot((gid == grp).astype(BF16), tri_ref[...])
        slot = jnp.zeros((1, tm), F32)
        for gg in range(N_GROUPS):
            slot = jnp.where(gid == gg, before[gg:gg + 1, :] + offsets[gg].astype(F32), slot)
        chunk = min(tm, 256)
        for c in range(tm // chunk):
            s_id = (lax.broadcasted_iota(jnp.int32, (chunk, tm), 0) + c * chunk).astype(F32)
            p_ref[c * chunk:(c + 1) * chunk, :] = (s_id == slot).astype(BF16)
        p = p_ref[...]
        xs_ref[...] = _dot(p, t_ref[...]).astype(BF16)
        c_hi = comb.astype(BF16)
        rest = comb - c_hi.astype(F32)
        c_mid = rest.astype(BF16)
        c_lo = (rest - c_mid.astype(F32)).astype(BF16)
        cs = _dot(p, jnp.concatenate([c_hi, c_mid, c_lo], axis=1))
        cs_ref[...] = cs[:, :LANES] + cs[:, LANES:2 * LANES] + cs[:, 2 * LANES:]
        ys_ref[...] = jnp.zeros_like(ys_ref)

    off_g = sum(jnp.where(g == gg, offsets[gg], 0) for gg in range(N_GROUPS))
    cnt_g = sum(jnp.where(g == gg, counts[gg], 0) for gg in range(N_GROUPS))
    first = (off_g // SUBLANE_PACK) * SUBLANE_PACK
    n_win = (off_g + cnt_g - first + win - 1) // win
    lane = lax.broadcasted_iota(jnp.int32, (win, LANES), 1)
    row = lax.broadcasted_iota(jnp.int32, (win, LANES), 0)

    def window(w, carry):
        want = first + w * win
        start = pl.multiple_of(jnp.minimum(want, tm - win), SUBLANE_PACK)
        x = xs_ref[pl.ds(start, win), :]
        c = jnp.where(row + start >= want, cs_ref[pl.ds(start, win), :], 0.0)
        acc = jnp.zeros((win, D_MODEL), F32)
        for e in range(EXPERTS_PER_GROUP):
            gate = _dot(x, wg_ref[e])
            up = _dot(x, wu_ref[e])
            hid = gate * jax.nn.sigmoid(gate) * up
            ce = jnp.sum(jnp.where(lane == g * EXPERTS_PER_GROUP + e, c, 0.0), axis=1, keepdims=True)
            acc = acc + ce * _dot(hid.astype(BF16), wd_ref[e])
        ys_ref[pl.ds(start, win), :] += acc
        return carry

    lax.fori_loop(0, n_win, window, 0)

    @pl.when(g == N_GROUPS - 1)
    def _():
        ys = ys_ref[...]
        y_hi = ys.astype(BF16)
        y_lo = (ys - y_hi.astype(F32)).astype(BF16)
        back = (((0,), (0,)), ((), ()))
        p = p_ref[...]
        o_ref[...] = (lax.dot_general(p, y_hi, back, preferred_element_type=F32)
                      + lax.dot_general(p, y_lo, back, preferred_element_type=F32))


def _moe(counts, t, comb, w_gate16, w_up16, w_down16, tm, sub_tiles):
    n = t.shape[0]
    win = min(320, tm)
    rr = lax.broadcasted_iota(jnp.int32, (tm, tm), 0)
    cc = lax.broadcasted_iota(jnp.int32, (tm, tm), 1)
    tri = (rr < cc).astype(BF16)
    row = lambda i, g, cnt: (i, 0)
    grp = lambda i, g, cnt: (g, 0, 0)
    grid_spec = pltpu.PrefetchScalarGridSpec(
        num_scalar_prefetch=1,
        grid=(n // tm, N_GROUPS),
        in_specs=[pl.BlockSpec((tm, D_MODEL), row), pl.BlockSpec((tm, LANES), row),
                  pl.BlockSpec((tm, tm), lambda i, g, cnt: (0, 0), pipeline_mode=pl.Buffered(1)),
                  pl.BlockSpec((EXPERTS_PER_GROUP, D_MODEL, D_EXPERT), grp),
                  pl.BlockSpec((EXPERTS_PER_GROUP, D_MODEL, D_EXPERT), grp),
                  pl.BlockSpec((EXPERTS_PER_GROUP, D_EXPERT, D_MODEL), grp)],
        out_specs=pl.BlockSpec((tm, D_MODEL), row),
        scratch_shapes=[pltpu.VMEM((tm, D_MODEL), BF16), pltpu.VMEM((tm, LANES), F32),
                        pltpu.VMEM((tm, D_MODEL), F32), pltpu.VMEM((tm, tm), BF16)])
    return pl.pallas_call(
        functools.partial(_moe_kernel, tm=tm, win=win, sub_tiles=sub_tiles),
        grid_spec=grid_spec,
        out_shape=jax.ShapeDtypeStruct((n, D_MODEL), F32),
        compiler_params=_cparams(("parallel", "arbitrary")),
        name="moe",
    )(counts, t, comb, tri, w_gate16, w_up16, w_down16)


def _ple_kernel(x_ref, moe_ref, p_ref, gp_ref, wg_ref, wp_ref, gf_ref, y_ref):
    x = x_ref[...] + moe_ref[...]
    gate = jax.nn.sigmoid(_dot(_rms(x, gp_ref[...]).astype(BF16), wg_ref[...]))
    x = x + gate * _dot(p_ref[...].astype(BF16), wp_ref[...])
    y_ref[...] = _rms(x, gf_ref[...])


def _ple(x, moe, p, g_ple, w_gate16, w_proj16, g_final, tm):
    n = x.shape[0]
    row = lambda i: (i, 0)
    const = lambda i: (0, 0)
    return pl.pallas_call(
        _ple_kernel,
        grid=(n // tm,),
        in_specs=[pl.BlockSpec((tm, D_MODEL), row), pl.BlockSpec((tm, D_MODEL), row),
                  pl.BlockSpec((tm, p.shape[1]), row),
                  pl.BlockSpec((1, D_MODEL), const), pl.BlockSpec(w_gate16.shape, const),
                  pl.BlockSpec(w_proj16.shape, const), pl.BlockSpec((1, D_MODEL), const)],
        out_specs=pl.BlockSpec((tm, D_MODEL), row),
        out_shape=jax.ShapeDtypeStruct((n, D_MODEL), F32),
        compiler_params=_cparams(("parallel",)),
        name="ple_final",
    )(x, moe, p, g_ple, w_gate16, w_proj16, g_final)


def _pick_tile(n, prefer):
    for t in prefer:
        if n % t == 0:
            return t
    raise ValueError(f"no row tile for {n} rows")


def kernel(x_prompt, x_sample, p_prompt, p_sample, cache_sb, cache_df, page_table, g_mix, w_in, lam_q1, lam_k1, lam_q2, lam_k2, g_subln, w_sb, w_df, w_o, g_ffn, w_grp, w_rtr, w_gate, w_up, w_down, g_ple, w_ple_gate, w_ple_proj, g_final):
    depth = g_mix.shape[0]
    assert depth == 1, "single-layer step only"
    batch, seq, _ = x_prompt.shape
    n_seq, dec, _ = x_sample.shape
    n_pages = page_table.shape[1]
    page = cache_sb.shape[2]
    past = n_pages * page
    lam_init = 0.8 - 0.6 * math.exp(-0.3 * 0)

    w_in16 = w_in[0].astype(BF16)
    w_sb16, w_df16, w_o16 = w_sb[0].astype(BF16), w_df[0].astype(BF16), w_o[0].astype(BF16)
    w_gate16, w_up16, w_down16 = w_gate[0].astype(BF16), w_up[0].astype(BF16), w_down[0].astype(BF16)
    w_pg16, w_pp16 = w_ple_gate[0].astype(BF16), w_ple_proj[0].astype(BF16)
    w_route = jnp.concatenate(
        [w_grp[0], jnp.transpose(w_rtr[0], (1, 0, 2)).reshape(D_MODEL, N_EXPERTS),
         jnp.zeros((D_MODEL, LANES - N_GROUPS - N_EXPERTS), F32)], axis=1)
    w_route_hi = w_route.astype(BF16)
    w_route_lo = (w_route - w_route_hi.astype(F32)).astype(BF16)
    w_route = jnp.concatenate([w_route_hi, w_route_lo, w_route_hi], axis=0)
    lam_p =jnp.concatenate([lam_q1, lam_k1, lam_q2, lam_k2], axis=0).astype(F32)
    g_final2 = g_final.reshape(1, D_MODEL)

    def finish(x, o_sb, o_df, gates, p):
        tm = _pick_tile(x.shape[0], (512, 256, 128, 64, 32, 16, 8))
        xm, t, comb, cnt = _merge(x, o_sb, o_df, gates, w_sb16, w_df16, w_o16, g_ffn, w_route, tm)
        tm_moe = _pick_tile(x.shape[0], (1024, 512, 256, 128, 64, 32, 16))
        moe = _moe(cnt[:, 0, :N_GROUPS], t, comb, w_gate16, w_up16, w_down16, tm_moe, tm_moe // tm)
        return _ple(xm, moe, p, g_ple, w_pg16, w_pp16, g_final2, tm)

    n_p = batch * seq
    tm_p = _pick_tile(seq, (256, 128))
    tq = _pick_tile(seq, (256, 128))
    xp = x_prompt.reshape(n_p, D_MODEL)
    tabs_p = _rope_tables(jnp.arange(seq, dtype=jnp.int32))
    q_sb, q_df, new_sb_p, new_df_p, sb16, df16, gates = _project(
        xp, g_mix, w_in16, tabs_p, seq // tm_p, BF16, tm_p, tokens_on_lanes_seq=seq)
    o_sb = _sb_prompt(q_sb, sb16, batch, seq, tq, 4)
    o_df = _df_prompt(lam_p, g_subln, q_df, df16, batch, seq, tq, 4, lam_init)
    y_prompt = finish(xp, o_sb, o_df, gates, p_prompt[0].reshape(n_p, -1)).reshape(x_prompt.shape)

    n_s = n_seq * dec
    tm_s = _pick_tile(n_s, (256, 128, 64, 32, 16, 8))
    xs = x_sample.reshape(n_s, D_MODEL)
    pos_s = past + jnp.arange(dec, dtype=jnp.int32)
    tabs_s = tuple(jnp.tile(t, (tm_s // dec, 1)) for t in _rope_tables(pos_s))
    q_sb, q_df, new_sb_s, new_df_s, _, _, gates = _project(xs, g_mix, w_in16, tabs_s, 1, F32, tm_s)
    pages_per_step = next(c for c in (8, 4, 2, 1) if n_pages % c == 0)
    o_sb, o_df = _sample_attention(page_table, lam_p, g_subln, q_sb, q_df, new_sb_s, new_df_s,
                                   cache_sb, cache_df, lam_init, pages_per_step)
    y_sample = finish(xs, o_sb, o_df, gates, p_sample[0].reshape(n_s, -1)).reshape(x_sample.shape)

    return (y_prompt, y_sample,
            jnp.transpose(new_sb_p.reshape(batch, 1, 2, SB_HEADS, HEAD_DIM, seq), (0, 1, 5, 2, 3, 4)),
            new_df_p.reshape(batch, 1, seq, 2, DF_HEADS, 2 * HEAD_DIM),
            new_sb_s.reshape(n_seq, 1, dec, 2, SB_HEADS, HEAD_DIM),
            new_df_s.reshape(n_seq, 1, dec, 2, DF_HEADS, 2 * HEAD_DIM))
```

```python
import functools
import math

import jax
import jax.numpy as jnp
from jax import lax
from jax.experimental import pallas as pl
from jax.experimental.pallas import tpu as pltpu

F32 = jnp.float32
BF16 = jnp.bfloat16

RMS_EPS = 1e-6
ROPE_THETA = 10000.0
D_MODEL = 1024
SB_HEADS = 8
DF_HEADS = 4
HEAD_DIM = 64
SB_WIDTH = SB_HEADS * HEAD_DIM
DF_WIDTH = DF_HEADS * 2 * HEAD_DIM
N_GROUPS = 4
EXPERTS_PER_GROUP = 8
N_EXPERTS = N_GROUPS * EXPERTS_PER_GROUP
D_EXPERT = 256
LANES = 128
NEG_BIG = -1e30
SCORE_SCALE = HEAD_DIM ** -0.5 * math.log2(math.e)
GROUP_LANE = N_EXPERTS
SUBLANE_PACK = 16
VMEM_LIMIT = 56 * 1024 * 1024


def _cparams(sem):
    return pltpu.CompilerParams(dimension_semantics=sem, vmem_limit_bytes=VMEM_LIMIT)


def _rms(x, g):
    ms = jnp.mean(x * x, axis=-1, keepdims=True)
    return x * lax.rsqrt(ms + RMS_EPS) * g


def _dot(a, b):
    return jnp.dot(a, b, preferred_element_type=F32)


def _dot_nt(a, b):
    return lax.dot_general(a, b, (((1,), (1,)), ((), ())), preferred_element_type=F32)


def _proj_kernel(x_ref, g_ref, w_ref, cos_ref, sa_ref, sb_ref,
                 qsb_ref, qdf_ref, nsb_ref, ndf_ref, sb16_ref, df16_ref, gate_ref, *, tokens_on_lanes):
    tm = x_ref.shape[0]
    h = _rms(x_ref[...], g_ref[...]).astype(BF16)

    def put_sb(part, t):
        if tokens_on_lanes:
            nsb_ref[part * SB_WIDTH:(part + 1) * SB_WIDTH, :] = t.T
        else:
            nsb_ref[:, part * SB_WIDTH:(part + 1) * SB_WIDTH] = t
        sb16_ref[:, part * SB_WIDTH:(part + 1) * SB_WIDTH] = t.astype(BF16)

    def put_df(part, t):
        for hd in range(DF_HEADS):
            ndf_ref[pl.ds(part * DF_HEADS + hd, tm, stride=2 * DF_HEADS), :] = t[:, hd * LANES:(hd + 1) * LANES]
        df16_ref[:, part * DF_WIDTH:(part + 1) * DF_WIDTH] = t.astype(BF16)

    cos, sa, sb = cos_ref[...], sa_ref[...], sb_ref[...]

    def mm(c):
        return _dot(h, w_ref[:, c * 1024:(c + 1) * 1024])

    def rope(t):
        parts = []
        for c in range(t.shape[1] // LANES):
            u = t[:, c * LANES:(c + 1) * LANES]
            parts.append(u * cos + pltpu.roll(u, 96, 1) * sa + pltpu.roll(u, 32, 1) * sb)
        return jnp.concatenate(parts, axis=1)

    z = mm(0)
    qsb_ref[...] = (z[:, :512] * SCORE_SCALE).astype(qsb_ref.dtype)
    put_sb(0, z[:, 512:])
    z = mm(1)
    put_sb(1, z[:, :512])
    qdf_ref[...] = (rope(z[:, 512:]) * SCORE_SCALE).astype(qdf_ref.dtype)
    z = mm(2)
    put_df(0, rope(z[:, :512]))
    put_df(1, z[:, 512:])
    gate_ref[:, :1024] = jax.nn.sigmoid(mm(3))
    gate_ref[:, 1024:] = jax.nn.sigmoid(mm(4))


def _project(x, g, w_in16, tables, period_blocks, q_dtype, tm, tokens_on_lanes_seq=None):
    n = x.shape[0]
    cos, sa, sb = tables
    row = lambda i: (i, 0)
    const = lambda i: (0, 0)
    tab = lambda i: (i % period_blocks, 0)
    if tokens_on_lanes_seq is None:
        nsb_shape, nsb_spec = (n, 2 * SB_WIDTH), pl.BlockSpec((tm, 2 * SB_WIDTH), row)
    else:
        per_seq = tokens_on_lanes_seq // tm
        nsb_shape = (n // tokens_on_lanes_seq, 2 * SB_WIDTH, tokens_on_lanes_seq)
        nsb_spec = pl.BlockSpec((None, 2 * SB_WIDTH, tm), lambda i: (i // per_seq, 0, i % per_seq))
    rows_df = 2 * DF_HEADS
    out_shape = (
        jax.ShapeDtypeStruct((n, 512), q_dtype), jax.ShapeDtypeStruct((n, 512), q_dtype),
        jax.ShapeDtypeStruct(nsb_shape, F32), jax.ShapeDtypeStruct((n * rows_df, 2 * HEAD_DIM), F32),
        jax.ShapeDtypeStruct((n, 1024), BF16), jax.ShapeDtypeStruct((n, 1024), BF16),
        jax.ShapeDtypeStruct((n, 2048), F32))
    return pl.pallas_call(
        functools.partial(_proj_kernel, tokens_on_lanes=tokens_on_lanes_seq is not None),
        grid=(n // tm,),
        in_specs=[pl.BlockSpec((tm, D_MODEL), row), pl.BlockSpec((1, D_MODEL), const),
                  pl.BlockSpec(w_in16.shape, const, pipeline_mode=pl.Buffered(1)),
                  pl.BlockSpec((tm, LANES), tab), pl.BlockSpec((tm, LANES), tab),
                  pl.BlockSpec((tm, LANES), tab)],
        out_specs=(pl.BlockSpec((tm, 512), row), pl.BlockSpec((tm, 512), row),
                   nsb_spec, pl.BlockSpec((tm * rows_df, 2 * HEAD_DIM), row),
                   pl.BlockSpec((tm, 1024), row), pl.BlockSpec((tm, 1024), row),
                   pl.BlockSpec((tm, 2048), row)),
        out_shape=out_shape,
        compiler_params=_cparams(("parallel",)),
        name="proj",
    )(x, g, w_in16, cos, sa, sb)


def _rope_tables(pos):
    half = HEAD_DIM // 2
    inv = 1.0 / (ROPE_THETA ** (jnp.arange(half, dtype=F32) * 2.0 / HEAD_DIM))
    ang = pos.astype(F32)[:, None] * inv[None, :]
    cos, sin = jnp.cos(ang), jnp.sin(ang)
    zero = jnp.zeros_like(sin)
    cos64 = jnp.concatenate([cos, cos], axis=1)
    sa64 = jnp.concatenate([-sin, zero], axis=1)
    sb64 = jnp.concatenate([zero, sin], axis=1)
    rep = lambda t: jnp.concatenate([t, t], axis=1)
    return rep(cos64), rep(sa64), rep(sb64)


def _sb_prompt_kernel(q_ref, k_ref, v_ref, o_ref, *, tq, cols):
    i = pl.program_id(2)
    lane = lax.broadcasted_iota(jnp.int32, (tq, LANES), 1)
    row = lax.broadcasted_iota(jnp.int32, (tq, tq), 0)
    col = lax.broadcasted_iota(jnp.int32, (tq, tq), 1)
    later = (row > col).astype(BF16)
    causal = col < row
    heads = []
    for cb in range(cols):
        q = q_ref[:, cb * LANES:(cb + 1) * LANES].astype(F32)
        heads.append((cb, jnp.where(lane < HEAD_DIM, q, 0.0).astype(BF16)))
        heads.append((cb, jnp.where(lane >= HEAD_DIM, q, 0.0).astype(BF16)))

    def tile(kb, state, masked):
        start = pl.multiple_of(kb * tq, tq)
        new = []
        for (cb, qh), (c, acc) in zip(heads, state):
            k = k_ref[pl.ds(start, tq), cb * LANES:(cb + 1) * LANES]
            v = v_ref[pl.ds(start, tq), cb * LANES:(cb + 1) * LANES]
            z = _dot_nt(qh, k)
            soft = jnp.log2(1.0 + jnp.exp2(-jnp.abs(z)))
            log_beta = jnp.minimum(z, 0.0) - soft
            log_1mb = log_beta - z
            if masked:
                log_1mb = jnp.where(causal, log_1mb, 0.0)
            rem = _dot(log_1mb.astype(BF16), later)
            w = jnp.exp2(log_beta + rem + c)
            if masked:
                w = jnp.where(causal, w, 0.0)
            acc = acc + _dot(w.astype(BF16), v)
            c = c + jnp.sum(log_1mb, axis=1, keepdims=True)
            new.append((c, acc))
        return tuple(new)

    init = tuple((jnp.zeros((tq, 1), F32), jnp.zeros((tq, LANES), F32)) for _ in heads)
    state = tile(i, init, True)
    state = lax.fori_loop(0, i, lambda s, st: tile(i - 1 - s, st, False), state)
    for cb in range(cols):
        o_ref[:, cb * LANES:(cb + 1) * LANES] = jnp.where(
            lane < HEAD_DIM, state[2 * cb][1], state[2 * cb + 1][1]).astype(o_ref.dtype)


def _sb_prompt(q, kv16, batch, seq, tq, cols):
    nq = seq // tq
    width = cols * LANES
    n_blk = SB_WIDTH // width
    return pl.pallas_call(
        functools.partial(_sb_prompt_kernel, tq=tq, cols=cols),
        grid=(batch, n_blk, nq),
        in_specs=[pl.BlockSpec((tq, width), lambda b, p, i: (b * nq + i, p)),
                  pl.BlockSpec((seq, width), lambda b, p, i: (b, p)),
                  pl.BlockSpec((seq, width), lambda b, p, i: (b, n_blk + p))],
        out_specs=pl.BlockSpec((tq, width), lambda b, p, i: (b * nq + i, p)),
        out_shape=jax.ShapeDtypeStruct((batch * seq, SB_WIDTH), BF16),
        compiler_params=_cparams(("parallel", "parallel", "arbitrary")),
        name="sb_prompt",
    )(q, kv16, kv16)


def _lambda(lam_ref, lam_init):
    lp = lam_ref[...]
    a = jnp.sum(lp[0:1] * lp[1:2], axis=1, keepdims=True)
    b = jnp.sum(lp[2:3] * lp[3:4], axis=1, keepdims=True)
    return jnp.exp(a) - jnp.exp(b) + lam_init


def _df_prompt_kernel(lam_ref, g_ref, q_ref, k_ref, v_ref, o_ref, *, tq, cols, lam_init):
    i = pl.program_id(2)
    lane = lax.broadcasted_iota(jnp.int32, (tq, LANES), 1)
    row = lax.broadcasted_iota(jnp.int32, (tq, tq), 0)
    col = lax.broadcasted_iota(jnp.int32, (tq, tq), 1)
    causal = col <= row
    halves = []
    for cb in range(cols):
        q = q_ref[:, cb * LANES:(cb + 1) * LANES].astype(F32)
        halves.append((cb, jnp.where(lane < HEAD_DIM, q, 0.0).astype(BF16)))
        halves.append((cb, jnp.where(lane >= HEAD_DIM, q, 0.0).astype(BF16)))

    ones = jnp.ones((tq, LANES), BF16)

    def tile(kb, st, masked):
        start = pl.multiple_of(kb * tq, tq)
        new = []
        for (cb, qx), (m, acc) in zip(halves, st):
            k = k_ref[pl.ds(start, tq), cb * LANES:(cb + 1) * LANES]
            v = v_ref[pl.ds(start, tq), cb * LANES:(cb + 1) * LANES]
            s = _dot_nt(qx, k)
            if masked:
                s = jnp.where(causal, s, NEG_BIG)
            m_new = jnp.maximum(m, jnp.max(s, axis=1, keepdims=True))
            alpha = jnp.exp2(m - m_new)
            p = jnp.exp2(s - m_new)
            acc = acc * alpha + _dot(p.astype(BF16), jnp.concatenate([v, ones], axis=1))
            new.append((m_new, acc))
        return tuple(new)

    init = (jnp.full((tq, 1), NEG_BIG, F32), jnp.zeros((tq, 2 * LANES), F32))
    st = tile(i, (init,) * len(halves), True)
    st = lax.fori_loop(0, i, lambda s, c: tile(i - 1 - s, c, False), st)
    lam = _lambda(lam_ref, lam_init)
    for cb in range(cols):
        (_, e1), (_, e2) = st[2 * cb], st[2 * cb + 1]
        o = e1[:, :LANES] / e1[:, LANES:] - lam * (e2[:, :LANES] / e2[:, LANES:])
        o_ref[:, cb * LANES:(cb + 1) * LANES] = (_rms(o, g_ref[...]) * (1.0 - lam_init)).astype(o_ref.dtype)


def _df_prompt(lam_p, g_subln, q, kv16, batch, seq, tq, cols, lam_init):
    nq = seq // tq
    width = cols * LANES
    n_blk = DF_WIDTH // width
    const = lambda b, h, i: (0, 0)
    return pl.pallas_call(
        functools.partial(_df_prompt_kernel, tq=tq, cols=cols, lam_init=lam_init),
        grid=(batch, n_blk, nq),
        in_specs=[pl.BlockSpec(lam_p.shape, const), pl.BlockSpec(g_subln.shape, const),
                  pl.BlockSpec((tq, width), lambda b, h, i: (b * nq + i, h)),
                  pl.BlockSpec((seq, width), lambda b, h, i: (b, h)),
                  pl.BlockSpec((seq, width), lambda b, h, i: (b, n_blk + h))],
        out_specs=pl.BlockSpec((tq, width), lambda b, h, i: (b * nq + i, h)),
        out_shape=jax.ShapeDtypeStruct((batch * seq, DF_WIDTH), BF16),
        compiler_params=_cparams(("parallel", "parallel", "arbitrary")),
        name="df_prompt",
    )(lam_p, g_subln, q, kv16, kv16)


def _sample_kernel(pt_ref, lam_ref, g_ref, later_ref, qsb_ref, qdf_ref, nsb_ref, ndf_ref, *rest,
                   pages_per_step, page, dec, lam_init):
    sb_pages = rest[:pages_per_step]
    df_pages = rest[pages_per_step:2 * pages_per_step]
    osb_ref, odf_ref = rest[2 * pages_per_step:2 * pages_per_step + 2]
    acc_sb, c_sb, acc_df, m_df, l_df = rest[2 * pages_per_step + 2:]
    j = pl.program_id(1)
    n_sb = SB_HEADS * dec
    n_df = DF_HEADS * 2 * dec
    lane = lax.broadcasted_iota(jnp.int32, (dec, LANES), 1)

    qrow = lax.broadcasted_iota(jnp.int32, (n_sb, SB_WIDTH), 0) // dec
    qcol = lax.broadcasted_iota(jnp.int32, (n_sb, SB_WIDTH), 1) // HEAD_DIM
    q_sb = jnp.where(qrow == qcol, jnp.concatenate([qsb_ref[...]] * SB_HEADS, axis=0), 0.0).astype(BF16)
    q_df = qdf_ref[...]
    q_df_heads = []
    for h in range(DF_HEADS):
        qh = q_df[:, h * LANES:(h + 1) * LANES]
        q_df_heads.append(jnp.concatenate(
            [jnp.where(lane < HEAD_DIM, qh, 0.0), jnp.where(lane >= HEAD_DIM, qh, 0.0)], axis=0).astype(BF16))

    def pad_tokens(t):
        if t.shape[0] == page:
            return t
        return jnp.concatenate([t, jnp.zeros((page - t.shape[0], t.shape[1]), t.dtype)], axis=0)

    def sb_weights(z, masked):
        n = z.shape[1] // page
        soft = jnp.log2(1.0 + jnp.exp2(-jnp.abs(z)))
        log_beta = jnp.minimum(z, 0.0) - soft
        log_1mb = log_beta - z
        if masked:
            rows = lax.broadcasted_iota(jnp.int32, z.shape, 0) % dec
            cols = lax.broadcasted_iota(jnp.int32, z.shape, 1)
            valid = cols < rows
            log_1mb = jnp.where(valid, log_1mb, 0.0)
        hi = log_1mb.astype(BF16).astype(F32)
        parts = [t[:, r * page:(r + 1) * page] for t in (hi, log_1mb - hi) for r in range(n)]
        both = _dot(jnp.concatenate(parts, axis=0).astype(BF16), later_ref[...])
        c = c_sb[...]
        ws = []
        for r in range(n):
            top = both[r * n_sb:(r + 1) * n_sb] + both[(n + r) * n_sb:(n + r + 1) * n_sb]
            ws.append(jnp.exp2(log_beta[:, r * page:(r + 1) * page] + top[:, :page] + c))
            c = c + top[:, page:]
        c_sb[...] = c
        w = ws[0] if n == 1 else jnp.concatenate(ws, axis=1)
        if masked:
            w = jnp.where(valid, w, 0.0)
        return w.astype(BF16)

    def sb_pages_step(refs):
        kt = jnp.concatenate([ref[:SB_WIDTH, :].astype(BF16) for ref in refs], axis=1)
        w = sb_weights(_dot(q_sb, kt), False)
        vt = jnp.concatenate([ref[SB_WIDTH:, :].astype(BF16) for ref in refs], axis=1)
        acc_sb[...] += _dot_nt(w, vt)

    def sb_new(ref):
        w = sb_weights(_dot_nt(q_sb, pad_tokens(ref[:, :SB_WIDTH]).astype(BF16)), True)
        acc_sb[...] += _dot(w, pad_tokens(ref[:, SB_WIDTH:]).astype(BF16))

    def df_block(refs, ntok, masked):
        def fetch(r):
            rows = [pad_tokens(ref[pl.ds(r, ntok, stride=2 * DF_HEADS), :]).astype(BF16) for ref in refs]
            return rows[0] if len(rows) == 1 else jnp.concatenate(rows, axis=0)
        s = jnp.concatenate([_dot_nt(q_df_heads[h], fetch(h)) for h in range(DF_HEADS)], axis=0)
        if masked:
            rows = lax.broadcasted_iota(jnp.int32, s.shape, 0) % dec
            cols = lax.broadcasted_iota(jnp.int32, s.shape, 1)
            s = jnp.where(cols <= rows, s, NEG_BIG)
        m_old = m_df[...]
        m_new = jnp.maximum(m_old, jnp.max(s, axis=1, keepdims=True))
        alpha = jnp.exp2(m_old - m_new)
        p = jnp.exp2(s - m_new)
        l_df[...] = l_df[...] * alpha + jnp.sum(p, axis=1, keepdims=True)
        pv = [_dot(p[h * 2 * dec:(h + 1) * 2 * dec].astype(BF16), fetch(DF_HEADS + h))
              for h in range(DF_HEADS)]
        acc_df[...] = acc_df[...] * alpha + jnp.concatenate(pv, axis=0)
        m_df[...] = m_new

    @pl.when(j == 0)
    def _():
        acc_sb[...] = jnp.zeros_like(acc_sb)
        c_sb[...] = jnp.zeros_like(c_sb)
        acc_df[...] = jnp.zeros_like(acc_df)
        m_df[...] = jnp.full_like(m_df, NEG_BIG)
        l_df[...] = jnp.zeros_like(l_df)
        sb_new(nsb_ref)
        df_block([ndf_ref], dec, True)

    sb_pages_step(sb_pages)
    df_block(df_pages, page, False)

    @pl.when(j == pl.num_programs(1) - 1)
    def _():
        a = acc_sb[...]
        osb_ref[...] = jnp.concatenate(
            [a[h * dec:(h + 1) * dec, h * HEAD_DIM:(h + 1) * HEAD_DIM] for h in range(SB_HEADS)],
            axis=1).astype(osb_ref.dtype)
        d = acc_df[...] / l_df[...]
        lam = _lambda(lam_ref, lam_init)
        outs = []
        for h in range(DF_HEADS):
            o = d[h * 2 * dec:h * 2 * dec + dec] - lam * d[h * 2 * dec + dec:(h + 1) * 2 * dec]
            outs.append(_rms(o, g_ref[...]) * (1.0 - lam_init))
        odf_ref[...] = jnp.concatenate(outs, axis=1).astype(odf_ref.dtype)


def _sample_attention(page_table, lam_p, g_subln, q_sb, q_df, new_sb, new_df, cache_sb, cache_df,
                      lam_init, pages_per_step):
    n_seq, n_pages = page_table.shape
    dec = q_sb.shape[0] // n_seq
    n_pool, _, page = cache_sb.shape[:3]
    sb_rows = 2 * SB_WIDTH
    df_rows = page * 2 * DF_HEADS
    csb = jnp.transpose(cache_sb, (0, 1, 3, 4, 5, 2)).reshape(n_pool, sb_rows, page)
    cdf = cache_df.reshape(n_pool, df_rows, 2 * HEAD_DIM)
    ndf = new_df.reshape(n_seq, dec * 2 * DF_HEADS, 2 * HEAD_DIM)
    jj = lax.broadcasted_iota(jnp.int32, (page, 2 * page), 0)
    ss = lax.broadcasted_iota(jnp.int32, (page, 2 * page), 1)
    later = ((jj > ss) | (ss >= page)).astype(BF16)
    n_steps = n_pages // pages_per_step
    const = lambda s, j, pt: (0, 0)
    seq_blk = lambda s, j, pt: (s, 0)

    def page_map(r):
        return lambda s, j, pt: (pt[s, n_pages - 1 - (j * pages_per_step + r)], 0, 0)

    in_specs = [pl.BlockSpec(lam_p.shape, const), pl.BlockSpec(g_subln.shape, const),
                pl.BlockSpec(later.shape, const),
                pl.BlockSpec((dec, SB_WIDTH), seq_blk), pl.BlockSpec((dec, DF_WIDTH), seq_blk),
                pl.BlockSpec((dec, 2 * SB_WIDTH), seq_blk),
                pl.BlockSpec((None,) + ndf.shape[1:], lambda s, j, pt: (s, 0, 0))]
    in_specs += [pl.BlockSpec((None, sb_rows, page), page_map(r)) for r in range(pages_per_step)]
    in_specs += [pl.BlockSpec((None, df_rows, 2 * HEAD_DIM), page_map(r)) for r in range(pages_per_step)]
    n_sb, n_df = SB_HEADS * dec, DF_HEADS * 2 * dec
    grid_spec = pltpu.PrefetchScalarGridSpec(
        num_scalar_prefetch=1,
        grid=(n_seq, n_steps),
        in_specs=in_specs,
        out_specs=(pl.BlockSpec((dec, SB_WIDTH), seq_blk), pl.BlockSpec((dec, DF_WIDTH), seq_blk)),
        scratch_shapes=[pltpu.VMEM((n_sb, SB_WIDTH), F32), pltpu.VMEM((n_sb, LANES), F32),
                        pltpu.VMEM((n_df, LANES), F32), pltpu.VMEM((n_df, 1), F32),
                        pltpu.VMEM((n_df, 1), F32)])
    return pl.pallas_call(
        functools.partial(_sample_kernel, pages_per_step=pages_per_step, page=page, dec=dec,
                          lam_init=lam_init),
        grid_spec=grid_spec,
        out_shape=(jax.ShapeDtypeStruct((n_seq * dec, SB_WIDTH), F32),
                   jax.ShapeDtypeStruct((n_seq * dec, DF_WIDTH), F32)),
        compiler_params=_cparams(("parallel", "arbitrary")),
        name="sample_attn",
    )(page_table, lam_p, g_subln, later, q_sb, q_df, new_sb, ndf,
      *([csb] * pages_per_step), *([cdf] * pages_per_step))


def _route(logits):
    lane = lax.broadcasted_iota(jnp.int32, logits.shape, 1).astype(F32)
    big = 1e9

    def first_argmax(vals, valid):
        v = jnp.where(valid, vals, -jnp.inf)
        top = jnp.max(v, axis=1, keepdims=True)
        idx = jnp.min(jnp.where(valid & (v == top), lane, big), axis=1, keepdims=True)
        return top, idx

    is_grp = lane < N_GROUPS
    g_top, g_idx = first_argmax(logits, is_grp)
    p_g = 1.0 / jnp.sum(jnp.where(is_grp, jnp.exp(logits - g_top), 0.0), axis=1, keepdims=True)
    lo = N_GROUPS + g_idx * EXPERTS_PER_GROUP
    in_grp = (lane >= lo) & (lane < lo + EXPERTS_PER_GROUP)
    v1, i1 = first_argmax(logits, in_grp)
    v2, i2 = first_argmax(logits, in_grp & (lane != i1))
    e2 = jnp.exp(v2 - v1)
    w1 = p_g / (1.0 + e2)
    w2 = p_g * e2 / (1.0 + e2)
    comb = jnp.where(lane == i1, w1, 0.0) + jnp.where(lane == i2, w2, 0.0)
    comb = pltpu.roll(comb, LANES - N_GROUPS, 1)
    comb = jnp.where(lane == GROUP_LANE, g_idx, comb)
    counts = jnp.sum(jnp.where(lane == g_idx, 1.0, 0.0), axis=0, keepdims=True)
    return comb, counts


def _merge_kernel(x_ref, osb_ref, odf_ref, gate_ref, wsb_ref, wdf_ref, wo_ref, g_ref, wr_ref,
                  xm_ref, t_ref, comb_ref, cnt_ref):
    a = _dot(osb_ref[...].astype(BF16), wsb_ref[...])
    b = _dot(odf_ref[...].astype(BF16), wdf_ref[...])
    merged = gate_ref[:, :D_MODEL] * a + gate_ref[:, D_MODEL:] * b
    xm = x_ref[...] + _dot(merged.astype(BF16), wo_ref[...])
    xm_ref[...] = xm
    t = _rms(xm, g_ref[...])
    t_hi = t.astype(BF16)
    t_ref[...] = t_hi
    t_lo = (t - t_hi.astype(F32)).astype(BF16)
    logits = _dot(jnp.concatenate([t_hi, t_hi, t_lo], axis=1), wr_ref[...])
    comb, counts = _route(logits)
    comb_ref[...] = comb
    cnt_ref[...] = jnp.broadcast_to(counts, cnt_ref.shape).astype(jnp.int32)


def _merge(x, o_sb, o_df, gates, w_sb16, w_df16, w_o16, g_ffn, w_route, tm):
    n = x.shape[0]
    row = lambda i: (i, 0)
    const = lambda i: (0, 0)
    return pl.pallas_call(
        _merge_kernel,
        grid=(n // tm,),
        in_specs=[pl.BlockSpec((tm, D_MODEL), row), pl.BlockSpec((tm, SB_WIDTH), row),
                  pl.BlockSpec((tm, DF_WIDTH), row), pl.BlockSpec((tm, 2 * D_MODEL), row),
                  pl.BlockSpec(w_sb16.shape, const), pl.BlockSpec(w_df16.shape, const),
                  pl.BlockSpec(w_o16.shape, const), pl.BlockSpec((1, D_MODEL), const),
                  pl.BlockSpec(w_route.shape, const)],
        out_specs=(pl.BlockSpec((tm, D_MODEL), row), pl.BlockSpec((tm, D_MODEL), row),
                   pl.BlockSpec((tm, LANES), row), pl.BlockSpec((None, 8, LANES), lambda i: (i, 0, 0))),
        out_shape=(jax.ShapeDtypeStruct((n, D_MODEL), F32), jax.ShapeDtypeStruct((n, D_MODEL), BF16),
                   jax.ShapeDtypeStruct((n, LANES), F32),
                   jax.ShapeDtypeStruct((n // tm, 8, LANES), jnp.int32)),
        compiler_params=_cparams(("parallel",)),
        name="merge_route",
    )(x, o_sb, o_df, gates, w_sb16, w_df16, w_o16, g_ffn, w_route)


def _moe_kernel(cnt_ref, t_ref, comb_ref, tri_ref, wg_ref, wu_ref, wd_ref, o_ref,
                xs_ref, cs_ref, ys_ref, p_ref, *, tm, win, sub_tiles):
    i = pl.program_id(0)
    g = pl.program_id(1)
    counts = [sum(cnt_ref[i * sub_tiles + s, gg] for s in range(sub_tiles)) for gg in range(N_GROUPS)]
    offsets = [sum(counts[:gg], jnp.int32(0)) for gg in range(N_GROUPS)]

    @pl.when(g == 0)
    def _():
        comb = comb_ref[...]
        gid = comb.T[GROUP_LANE:GROUP_LANE + 1, :]
        grp = lax.broadcasted_iota(jnp.int32, (8, tm), 0).astype(F32)
        before = _dot((gid == grp).astype(BF16), tri_ref[...])
        slot = jnp.zeros((1, tm), F32)
        for gg in range(N_GROUPS):
            slot = jnp.where(gid == gg, before[gg:gg + 1, :] + offsets[gg].astype(F32), slot)
        chunk = min(tm, 256)
        for c in range(tm // chunk):
            s_id = (lax.broadcasted_iota(jnp.int32, (chunk, tm), 0) + c * chunk).astype(F32)
            p_ref[c * chunk:(c + 1) * chunk, :] = (s_id == slot).astype(BF16)
        p = p_ref[...]
        xs_ref[...] = _dot(p, t_ref[...]).astype(BF16)
        c_hi = comb.astype(BF16)
        rest = comb - c_hi.astype(F32)
        c_mid = rest.astype(BF16)
        c_lo = (rest - c_mid.astype(F32)).astype(BF16)
        cs = _dot(p, jnp.concatenate([c_hi, c_mid, c_lo], axis=1))
        cs_ref[...] = cs[:, :LANES] + cs[:, LANES:2 * LANES] + cs[:, 2 * LANES:]
        ys_ref[...] = jnp.zeros_like(ys_ref)

    off_g = sum(jnp.where(g == gg, offsets[gg], 0) for gg in range(N_GROUPS))
    cnt_g = sum(jnp.where(g == gg, counts[gg], 0) for gg in range(N_GROUPS))
    first = (off_g // SUBLANE_PACK) * SUBLANE_PACK
    n_win = (off_g + cnt_g - first + win - 1) // win
    lane = lax.broadcasted_iota(jnp.int32, (win, LANES), 1)
    row = lax.broadcasted_iota(jnp.int32, (win, LANES), 0)

    def window(w, carry):
        want = first + w * win
        start = pl.multiple_of(jnp.minimum(want, tm - win), SUBLANE_PACK)
        x = xs_ref[pl.ds(start, win), :]
        c = jnp.where(row + start >= want, cs_ref[pl.ds(start, win), :], 0.0)
        acc = jnp.zeros((win, D_MODEL), F32)
        for e in range(EXPERTS_PER_GROUP):
            gate = _dot(x, wg_ref[e])
            up = _dot(x, wu_ref[e])
            hid = gate * jax.nn.sigmoid(gate) * up
            ce = jnp.sum(jnp.where(lane == g * EXPERTS_PER_GROUP + e, c, 0.0), axis=1, keepdims=True)
            acc = acc + ce * _dot(hid.astype(BF16), wd_ref[e])
        ys_ref[pl.ds(start, win), :] += acc
        return carry

    lax.fori_loop(0, n_win, window, 0)

    @pl.when(g == N_GROUPS - 1)
    def _():
        ys = ys_ref[...]
        y_hi = ys.astype(BF16)
        y_lo = (ys - y_hi.astype(F32)).astype(BF16)
        back = (((0,), (0,)), ((), ()))
        p = p_ref[...]
        o_ref[...] = (lax.dot_general(p, y_hi, back, preferred_element_type=F32)
                      + lax.dot_general(p, y_lo, back, preferred_element_type=F32))


def _moe(counts, t, comb, w_gate16, w_up16, w_down16, tm, sub_tiles):
    n = t.shape[0]
    win = min(320, tm)
    rr = lax.broadcasted_iota(jnp.int32, (tm, tm), 0)
    cc = lax.broadcasted_iota(jnp.int32, (tm, tm), 1)
    tri = (rr < cc).astype(BF16)
    row = lambda i, g, cnt: (i, 0)
    grp = lambda i, g, cnt: (g, 0, 0)
    grid_spec = pltpu.PrefetchScalarGridSpec(
        num_scalar_prefetch=1,
        grid=(n // tm, N_GROUPS),
        in_specs=[pl.BlockSpec((tm, D_MODEL), row), pl.BlockSpec((tm, LANES), row),
                  pl.BlockSpec((tm, tm), lambda i, g, cnt: (0, 0), pipeline_mode=pl.Buffered(1)),
                  pl.BlockSpec((EXPERTS_PER_GROUP, D_MODEL, D_EXPERT), grp),
                  pl.BlockSpec((EXPERTS_PER_GROUP, D_MODEL, D_EXPERT), grp),
                  pl.BlockSpec((EXPERTS_PER_GROUP, D_EXPERT, D_MODEL), grp)],
        out_specs=pl.BlockSpec((tm, D_MODEL), row),
        scratch_shapes=[pltpu.VMEM((tm, D_MODEL), BF16), pltpu.VMEM((tm, LANES), F32),
                        pltpu.VMEM((tm, D_MODEL), F32), pltpu.VMEM((tm, tm), BF16)])
    return pl.pallas_call(
        functools.partial(_moe_kernel, tm=tm, win=win, sub_tiles=sub_tiles),
        grid_spec=grid_spec,
        out_shape=jax.ShapeDtypeStruct((n, D_MODEL), F32),
        compiler_params=_cparams(("parallel", "arbitrary")),
        name="moe",
    )(counts, t, comb, tri, w_gate16, w_up16, w_down16)


def _ple_kernel(x_ref, moe_ref, p_ref, gp_ref, wg_ref, wp_ref, gf_ref, y_ref):
    x = x_ref[...] + moe_ref[...]
    gate = jax.nn.sigmoid(_dot(_rms(x, gp_ref[...]).astype(BF16), wg_ref[...]))
    x = x + gate * _dot(p_ref[...].astype(BF16), wp_ref[...])
    y_ref[...] = _rms(x, gf_ref[...])


def _ple(x, moe, p, g_ple, w_gate16, w_proj16, g_final, tm):
    n = x.shape[0]
    row = lambda i: (i, 0)
    const = lambda i: (0, 0)
    return pl.pallas_call(
        _ple_kernel,
        grid=(n // tm,),
        in_specs=[pl.BlockSpec((tm, D_MODEL), row), pl.BlockSpec((tm, D_MODEL), row),
                  pl.BlockSpec((tm, p.shape[1]), row),
                  pl.BlockSpec((1, D_MODEL), const), pl.BlockSpec(w_gate16.shape, const),
                  pl.BlockSpec(w_proj16.shape, const), pl.BlockSpec((1, D_MODEL), const)],
        out_specs=pl.BlockSpec((tm, D_MODEL), row),
        out_shape=jax.ShapeDtypeStruct((n, D_MODEL), F32),
        compiler_params=_cparams(("parallel",)),
        name="ple_final",
    )(x, moe, p, g_ple, w_gate16, w_proj16, g_final)


def _pick_tile(n, prefer):
    for t in prefer:
        if n % t == 0:
            return t
    raise ValueError(f"no row tile for {n} rows")


def kernel(x_prompt, x_sample, p_prompt, p_sample, cache_sb, cache_df, page_table, g_mix, w_in, lam_q1, lam_k1, lam_q2, lam_k2, g_subln, w_sb, w_df, w_o, g_ffn, w_grp, w_rtr, w_gate, w_up, w_down, g_ple, w_ple_gate, w_ple_proj, g_final):
    depth = g_mix.shape[0]
    assert depth == 1, "single-layer step only"
    batch, seq, _ = x_prompt.shape
    n_seq, dec, _ = x_sample.shape
    n_pages = page_table.shape[1]
    page = cache_sb.shape[2]
    past = n_pages * page
    lam_init = 0.8 - 0.6 * math.exp(-0.3 * 0)

    w_in16 = w_in[0].astype(BF16)
    w_sb16, w_df16, w_o16 = w_sb[0].astype(BF16), w_df[0].astype(BF16), w_o[0].astype(BF16)
    w_gate16, w_up16, w_down16 = w_gate[0].astype(BF16), w_up[0].astype(BF16), w_down[0].astype(BF16)
    w_pg16, w_pp16 = w_ple_gate[0].astype(BF16), w_ple_proj[0].astype(BF16)
    w_route = jnp.concatenate(
        [w_grp[0], jnp.transpose(w_rtr[0], (1, 0, 2)).reshape(D_MODEL, N_EXPERTS),
         jnp.zeros((D_MODEL, LANES - N_GROUPS - N_EXPERTS), F32)], axis=1)
    w_route_hi = w_route.astype(BF16)
    w_route_lo = (w_route - w_route_hi.astype(F32)).astype(BF16)
    w_route = jnp.concatenate([w_route_hi, w_route_lo, w_route_hi], axis=0)
    lam_p =jnp.concatenate([lam_q1, lam_k1, lam_q2, lam_k2], axis=0).astype(F32)
    g_final2 = g_final.reshape(1, D_MODEL)

    def finish(x, o_sb, o_df, gates, p):
        tm = _pick_tile(x.shape[0], (512, 256, 128, 64, 32, 16, 8))
        xm, t, comb, cnt = _merge(x, o_sb, o_df, gates, w_sb16, w_df16, w_o16, g_ffn, w_route, tm)
        tm_moe = _pick_tile(x.shape[0], (1024, 512, 256, 128, 64, 32, 16))
        moe = _moe(cnt[:, 0, :N_GROUPS], t, comb, w_gate16, w_up16, w_down16, tm_moe, tm_moe // tm)
        return _ple(xm, moe, p, g_ple, w_pg16, w_pp16, g_final2, tm)

    n_p = batch * seq
    tm_p = _pick_tile(seq, (256, 128))
    tq = _pick_tile(seq, (256, 128))
    xp = x_prompt.reshape(n_p, D_MODEL)
    tabs_p = _rope_tables(jnp.arange(seq, dtype=jnp.int32))
    q_sb, q_df, new_sb_p, new_df_p, sb16, df16, gates = _project(
        xp, g_mix, w_in16, tabs_p, seq // tm_p, BF16, tm_p, tokens_on_lanes_seq=seq)
    o_sb = _sb_prompt(q_sb, sb16, batch, seq, tq, 4)
    o_df = _df_prompt(lam_p, g_subln, q_df, df16, batch, seq, tq, 4, lam_init)
    y_prompt = finish(xp, o_sb, o_df, gates, p_prompt[0].reshape(n_p, -1)).reshape(x_prompt.shape)

    n_s = n_seq * dec
    tm_s = _pick_tile(n_s, (256, 128, 64, 32, 16, 8))
    xs = x_sample.reshape(n_s, D_MODEL)
    pos_s = past + jnp.arange(dec, dtype=jnp.int32)
    tabs_s = tuple(jnp.tile(t, (tm_s // dec, 1)) for t in _rope_tables(pos_s))
    q_sb, q_df, new_sb_s, new_df_s, _, _, gates = _project(xs, g_mix, w_in16, tabs_s, 1, F32, tm_s)
    pages_per_step = next(c for c in (16, 8, 4, 2, 1) if n_pages % c == 0)
    o_sb, o_df = _sample_attention(page_table, lam_p, g_subln, q_sb, q_df, new_sb_s, new_df_s,
                                   cache_sb, cache_df, lam_init, pages_per_step)
    y_sample = finish(xs, o_sb, o_df, gates, p_sample[0].reshape(n_s, -1)).reshape(x_sample.shape)

    return (y_prompt, y_sample,
            jnp.transpose(new_sb_p.reshape(batch, 1, 2, SB_HEADS, HEAD_DIM, seq), (0, 1, 5, 2, 3, 4)),
            new_df_p.reshape(batch, 1, seq, 2, DF_HEADS, 2 * HEAD_DIM),
            new_sb_s.reshape(n_seq, 1, dec, 2, SB_HEADS, HEAD_DIM),
            new_df_s.reshape(n_seq, 1, dec, 2, DF_HEADS, 2 * HEAD_DIM))
```
